```python
import jax, jax.numpy as jnp
from jax import lax
import numpy as np

D_MODEL = 1024
BATCH = 1
SEQ = 16384
DEPTH = 4
DEC_BATCH = 32
DEC_SEQ = 16
PAST_LEN = 4096

CHUNK = 64
N_META = 16
Q_BLOCK = 128
EPS = 1e-6
F32 = jnp.float32

MLA_HEADS = 8
QK_NOPE = 64
QK_ROPE = 32
V_HEAD = 64
Q_LORA = 384
KV_LORA = 256
ROPE_THETA = 10000.0
MLA_SCALE = (QK_NOPE + QK_ROPE) ** -0.5
MLA_WIDTH = MLA_HEADS * V_HEAD

GDN_HEADS = 4
GDN_DK = 128
GDN_DV = 128
CONV_W = 4
GDN_WIDTH = GDN_HEADS * GDN_DV
GDN_QKV = GDN_HEADS * (2 * GDN_DK + GDN_DV)

MIX_WIDTH = MLA_WIDTH + GDN_WIDTH
D_FF = -(-8 * D_MODEL // (3 * 256)) * 256

OFF_KV = Q_LORA
OFF_PE = OFF_KV + KV_LORA
OFF_QKV = OFF_PE + QK_ROPE
OFF_Z = OFF_QKV + GDN_QKV
OFF_B = OFF_Z + GDN_WIDTH
OFF_A = OFF_B + GDN_HEADS
IN_WIDTH = OFF_A + GDN_HEADS
IN_OFFSETS = (OFF_KV, OFF_PE, OFF_QKV, OFF_Z, OFF_B, OFF_A)

kernel_name = 'hybrid_mla_gdn_streaming_step'


def rmsnorm(x, w):
    xf = x.astype(F32)
    y = xf * lax.rsqrt(jnp.mean(xf * xf, axis=-1, keepdims=True) + EPS)
    return (y * w.astype(F32)).astype(x.dtype)


def l2norm(x):
    xf = x.astype(F32)
    return xf * lax.rsqrt(jnp.sum(xf * xf, axis=-1, keepdims=True) + EPS)


def rope(x, pos):
    half = QK_ROPE // 2
    inv = ROPE_THETA ** (-jnp.arange(half, dtype=F32) / half)
    ang = pos[:, None] * inv[None, :]
    c = jnp.cos(ang)[None, :, None, :]
    s = jnp.sin(ang)[None, :, None, :]
    xf = x.astype(F32)
    x1, x2 = xf[..., :half], xf[..., half:]
    return jnp.concatenate([x1 * c - x2 * s, x1 * s + x2 * c], axis=-1).astype(x.dtype)


def causal_conv(xpad, w):
    L = xpad.shape[1] - (CONV_W - 1)
    y = xpad[:, 0:L] * w[0]
    for i in range(1, CONV_W):
        y = y + xpad[:, i:i + L] * w[i]
    return jax.nn.silu(y)


def mixer_front(x, pre_w, w_in_l, q_norm_l, kv_norm_l, w_uq_l):
    B, L, _ = x.shape
    xn = rmsnorm(x, pre_w)
    c_q, c_kv, k_rope, qkv, z, b, a = jnp.split(xn @ w_in_l, IN_OFFSETS, axis=-1)
    q = (rmsnorm(c_q, q_norm_l) @ w_uq_l).reshape(B, L, MLA_HEADS, QK_NOPE + QK_ROPE)
    return q, rmsnorm(c_kv, kv_norm_l), k_rope, qkv, z, b, a


def mla_prompt(q, c_kv, k_rope, w_uk_l, w_uv_l, pos, cid):
    B, L = q.shape[:2]
    q_pe = rope(q[..., QK_NOPE:], pos)
    k_pe = rope(k_rope[:, :, None, :], pos)
    k_nope = jnp.einsum('blc,chd->blhd', c_kv, w_uk_l)
    v = jnp.einsum('blc,chd->blhd', c_kv, w_uv_l)
    qf = jnp.concatenate([q[..., :QK_NOPE], q_pe], axis=-1)
    kf = jnp.concatenate([k_nope, jnp.broadcast_to(k_pe, (B, L, MLA_HEADS, QK_ROPE))], axis=-1)
    s_m = jnp.einsum('bqhd,bkhd->bhqk', qf[:, :N_META], kf[:, :N_META]).astype(F32) * MLA_SCALE
    o_m = jnp.einsum('bhqk,bkhd->bqhd', jax.nn.softmax(s_m, axis=-1).astype(v.dtype), v[:, :N_META])
    nq = (L - N_META) // Q_BLOCK
    qb = qf[:, N_META:].reshape(B, nq, Q_BLOCK, MLA_HEADS, QK_NOPE + QK_ROPE).transpose(1, 0, 2, 3, 4)
    qcid = cid[N_META:].reshape(nq, Q_BLOCK)

    def attend(args):
        qblk, qc = args
        s = jnp.einsum('bqhd,bkhd->bhqk', qblk, kf).astype(F32) * MLA_SCALE
        s = jnp.where((cid[None, :] <= qc[:, None])[None, None], s, -jnp.inf)
        p = jax.nn.softmax(s, axis=-1).astype(v.dtype)
        return jnp.einsum('bhqk,bkhd->bqhd', p, v)

    o_f = lax.map(attend, (qb, qcid))
    o_f = o_f.transpose(1, 0, 2, 3, 4).reshape(B, L - N_META, MLA_HEADS, V_HEAD)
    return jnp.concatenate([o_m, o_f], axis=1), k_pe[:, :, 0]


def mla_sample(q, c_kv, k_rope, cache_lat, cache_pe, w_uk_l, w_uv_l, pos):
    q_pe = rope(q[..., QK_NOPE:], pos)
    k_pe = rope(k_rope[:, :, None, :], pos)[:, :, 0]
    cache_lat = cache_lat.astype(c_kv.dtype)
    cache_pe = cache_pe.astype(k_pe.dtype)
    q_lat = jnp.einsum('bthd,chd->bthc', q[..., :QK_NOPE], w_uk_l)
    s_past = jnp.einsum('bthc,bsc->bhts', q_lat, cache_lat) + jnp.einsum('bthr,bsr->bhts', q_pe, cache_pe)
    s_new = jnp.einsum('bthc,bsc->bhts', q_lat, c_kv) + jnp.einsum('bthr,bsr->bhts', q_pe, k_pe)
    s = jnp.concatenate([s_past, s_new], axis=-1).astype(F32) * MLA_SCALE
    p = jax.nn.softmax(s, axis=-1).astype(c_kv.dtype)
    P = cache_lat.shape[1]
    o_lat = jnp.einsum('bhts,bsc->bthc', p[..., :P], cache_lat) + jnp.einsum('bhts,bsc->bthc', p[..., P:], c_kv)
    return jnp.einsum('bthc,chd->bthd', o_lat, w_uv_l), k_pe


def gdn_prepare(u, b, a, a_log_l, dt_bias_l):
    B, L, _ = u.shape
    q, k, v = jnp.split(u.astype(F32), [GDN_HEADS * GDN_DK, 2 * GDN_HEADS * GDN_DK], axis=-1)
    q = l2norm(q.reshape(B, L, GDN_HEADS, GDN_DK)) * (GDN_DK ** -0.5)
    k = l2norm(k.reshape(B, L, GDN_HEADS, GDN_DK))
    v = v.reshape(B, L, GDN_HEADS, GDN_DV)
    beta = jax.nn.sigmoid(b.astype(F32))
    g = -jnp.exp(a_log_l.astype(F32)) * jax.nn.softplus(a.astype(F32) + dt_bias_l.astype(F32))
    return q, k, v, g, beta


def gdn_chunked(q, k, v, g, beta, s0, chunk):
    B, L, H, DK = q.shape
    N = L // chunk
    blk = lambda t: t.reshape((B, N, chunk) + t.shape[2:]).transpose((1, 0, 3, 2) + tuple(range(4, t.ndim + 1)))
    q, k, v, g, beta = blk(q), blk(k), blk(v), blk(g), blk(beta)
    gc = jnp.cumsum(g, axis=-1)
    tri = jnp.tril(jnp.ones((chunk, chunk), bool))
    strict = jnp.tril(jnp.ones((chunk, chunk), bool), k=-1)
    decay = jnp.exp(jnp.where(tri, gc[..., :, None] - gc[..., None, :], -jnp.inf))
    kk = jnp.einsum('nbhid,nbhjd->nbhij', k, k)
    m = jnp.where(strict, beta[..., :, None] * kk * decay, 0.0)
    eye = jnp.eye(chunk, dtype=F32)
    t_mat = lax.linalg.triangular_solve(eye + m, jnp.broadcast_to(eye, m.shape), left_side=True, lower=True)
    u = jnp.einsum('nbhij,nbhjd->nbhid', t_mat, v * beta[..., None])
    w = jnp.einsum('nbhij,nbhjd->nbhid', t_mat, k * (beta * jnp.exp(gc))[..., None])
    qk = jnp.einsum('nbhid,nbhjd->nbhij', q, k) * decay
    qg = q * jnp.exp(gc)[..., None]
    kd = k * jnp.exp(gc[..., -1:] - gc)[..., None]
    glast = jnp.exp(gc[..., -1])

    def step(S, xs):
        u_c, w_c, qk_c, qg_c, kd_c, gl_c = xs
        v_new = u_c - jnp.einsum('bhcd,bhde->bhce', w_c, S)
        o = jnp.einsum('bhcd,bhde->bhce', qg_c, S) + jnp.einsum('bhij,bhje->bhie', qk_c, v_new)
        S = S * gl_c[..., None, None] + jnp.einsum('bhcd,bhce->bhde', kd_c, v_new)
        return S, o

    S, o = lax.scan(step, s0, (u, w, qk, qg, kd, glast))
    o = o.transpose(1, 0, 3, 2, 4).reshape(B, L, H, o.shape[-1])
    return o, S


def gdn_output(o, z, norm_l):
    B, L = o.shape[:2]
    zf = z.astype(F32).reshape(B, L, GDN_HEADS, GDN_DV)
    return (rmsnorm(o, norm_l) * jax.nn.silu(zf)).reshape(B, L, GDN_WIDTH)


def mixer_back(x, o_mla, o_gdn, w_o_l, post_mix_l, pre_ffn_l, w_gate_l, w_up_l, w_down_l, post_ffn_l):
    B, L, _ = x.shape
    mix = jnp.concatenate([o_mla.reshape(B, L, MLA_WIDTH).astype(x.dtype), o_gdn.astype(x.dtype)], axis=-1)
    h = x + rmsnorm(mix @ w_o_l, post_mix_l)
    hn = rmsnorm(h, pre_ffn_l)
    f = (jax.nn.silu(hn @ w_gate_l) * (hn @ w_up_l)) @ w_down_l
    return h + rmsnorm(f, post_ffn_l)


def setup_inputs(seed: int = 0) -> dict:
    key = jax.random.key(seed)
    k = jax.random.split(key, 32)
    nrm = lambda i, shape, scale: jax.random.normal(k[i], shape, F32) * scale
    gain = lambda i, shape: 1.0 + 0.02 * jax.random.normal(k[i], shape, F32)
    dt = jnp.exp(jax.random.uniform(k[14], (DEPTH, GDN_HEADS), F32, np.log(1e-3), np.log(1e-1)))
    return {
        'x_prompt': nrm(0, (BATCH, SEQ, D_MODEL), 1.0),
        'x_sample': nrm(1, (DEC_BATCH, DEC_SEQ, D_MODEL), 1.0),
        'cache_mla_latent': nrm(2, (DEPTH, DEC_BATCH, PAST_LEN, KV_LORA), 1.0),
        'cache_mla_krope': nrm(3, (DEPTH, DEC_BATCH, PAST_LEN, QK_ROPE), 1.0),
        'state_gdn': nrm(4, (DEPTH, DEC_BATCH, GDN_HEADS, GDN_DK, GDN_DV), 0.1),
        'state_gdn_conv': nrm(5, (DEPTH, DEC_BATCH, CONV_W - 1, GDN_QKV), 1.0),
        'meta_tokens': nrm(6, (N_META, D_MODEL), 1.0),
        'pre_mix_norm': gain(7, (DEPTH, D_MODEL)),
        'w_in': nrm(8, (DEPTH, D_MODEL, IN_WIDTH), D_MODEL ** -0.5),
        'q_norm': gain(9, (DEPTH, Q_LORA)),
        'kv_norm': gain(10, (DEPTH, KV_LORA)),
        'w_uq': nrm(11, (DEPTH, Q_LORA, MLA_HEADS * (QK_NOPE + QK_ROPE)), Q_LORA ** -0.5),
        'w_uk': nrm(12, (DEPTH, KV_LORA, MLA_HEADS, QK_NOPE), KV_LORA ** -0.5),
        'w_uv': nrm(13, (DEPTH, KV_LORA, MLA_HEADS, V_HEAD), KV_LORA ** -0.5),
        'conv_w': nrm(15, (DEPTH, CONV_W, GDN_QKV), CONV_W ** -0.5),
        'a_log': jnp.log(jax.random.uniform(k[16], (DEPTH, GDN_HEADS), F32, 1.0, 16.0)),
        'dt_bias': jnp.log(jnp.expm1(dt)),
        'gdn_norm': gain(17, (DEPTH, GDN_DV)),
        'w_o': nrm(18, (DEPTH, MIX_WIDTH, D_MODEL), MIX_WIDTH ** -0.5),
        'post_mix_norm': gain(19, (DEPTH, D_MODEL)),
        'pre_ffn_norm': gain(20, (DEPTH, D_MODEL)),
        'w_gate': nrm(21, (DEPTH, D_MODEL, D_FF), D_MODEL ** -0.5),
        'w_up': nrm(22, (DEPTH, D_MODEL, D_FF), D_MODEL ** -0.5),
        'w_down': nrm(23, (DEPTH, D_FF, D_MODEL), D_FF ** -0.5),
        'post_ffn_norm': gain(24, (DEPTH, D_MODEL)),
    }


def reference(x_prompt, x_sample, cache_mla_latent, cache_mla_krope, state_gdn, state_gdn_conv,
              meta_tokens, pre_mix_norm, w_in, q_norm, kv_norm, w_uq, w_uk, w_uv,
              conv_w, a_log, dt_bias, gdn_norm, w_o, post_mix_norm,
              pre_ffn_norm, w_gate, w_up, w_down, post_ffn_norm):
    B = x_prompt.shape[0]
    T = x_sample.shape[1]
    xp = jnp.concatenate([jnp.broadcast_to(meta_tokens.astype(x_prompt.dtype)[None], (B, N_META, D_MODEL)), x_prompt], axis=1)
    L = xp.shape[1]
    idx = jnp.arange(L)
    cid = jnp.where(idx < N_META, -1, (idx - N_META) // CHUNK)
    pos_p = idx.astype(F32)
    pos_s = (N_META + PAST_LEN + jnp.arange(T)).astype(F32)
    gpad = (-N_META) % CHUNK
    padf = lambda t: jnp.pad(t, ((0, 0), (gpad, 0)) + ((0, 0),) * (t.ndim - 2))
    xs = x_sample
    p_lat, p_pe, p_gdn, p_conv = [], [], [], []
    s_lat, s_pe, s_gdn, s_conv = [], [], [], []
    for l in range(DEPTH):
        q, ckv, kr, qkv, z, b, a = mixer_front(xp, pre_mix_norm[l], w_in[l], q_norm[l], kv_norm[l], w_uq[l])
        o_mla, kpe = mla_prompt(q, ckv, kr, w_uk[l], w_uv[l], pos_p, cid)
        qkv_pad = jnp.pad(qkv, ((0, 0), (CONV_W - 1, 0), (0, 0)))
        qg, kg, vg, gg, bg = gdn_prepare(causal_conv(qkv_pad, conv_w[l]), b, a, a_log[l], dt_bias[l])
        o_g, s_fin = gdn_chunked(padf(qg), padf(kg), padf(vg), padf(gg), padf(bg),
                                 jnp.zeros((B, GDN_HEADS, GDN_DK, GDN_DV), F32), CHUNK)
        o_g = gdn_output(o_g[:, gpad:], z, gdn_norm[l])
        xp = mixer_back(xp, o_mla, o_g, w_o[l], post_mix_norm[l], pre_ffn_norm[l], w_gate[l], w_up[l], w_down[l], post_ffn_norm[l])
        p_lat.append(ckv)
        p_pe.append(kpe)
        p_gdn.append(s_fin)
        p_conv.append(qkv_pad[:, -(CONV_W - 1):])
        q, ckv, kr, qkv, z, b, a = mixer_front(xs, pre_mix_norm[l], w_in[l], q_norm[l], kv_norm[l], w_uq[l])
        o_mla, kpe = mla_sample(q, ckv, kr, cache_mla_latent[l], cache_mla_krope[l], w_uk[l], w_uv[l], pos_s)
        hist = jnp.concatenate([state_gdn_conv[l].astype(qkv.dtype), qkv], axis=1)
        qg, kg, vg, gg, bg = gdn_prepare(causal_conv(hist, conv_w[l]), b, a, a_log[l], dt_bias[l])
        o_g, s_new = gdn_chunked(qg, kg, vg, gg, bg, state_gdn[l].astype(F32), T)
        o_g = gdn_output(o_g, z, gdn_norm[l])
        xs = mixer_back(xs, o_mla, o_g, w_o[l], post_mix_norm[l], pre_ffn_norm[l], w_gate[l], w_up[l], w_down[l], post_ffn_norm[l])
        s_lat.append(ckv)
        s_pe.append(kpe)
        s_gdn.append(s_new)
        s_conv.append(hist[:, -(CONV_W - 1):])
    y_prompt = xp[:, N_META:]
    y_sample = xs
    new_prompt_latent = jnp.stack(p_lat)
    new_prompt_krope = jnp.stack(p_pe)
    new_prompt_gdn = jnp.stack(p_gdn)
    new_prompt_conv = jnp.stack(p_conv)
    new_sample_latent = jnp.stack(s_lat)
    new_sample_krope = jnp.stack(s_pe)
    new_sample_gdn = jnp.stack(s_gdn)
    new_sample_conv = jnp.stack(s_conv)
    return (y_prompt, y_sample, new_prompt_latent, new_prompt_krope, new_prompt_gdn, new_prompt_conv,
            new_sample_latent, new_sample_krope, new_sample_gdn, new_sample_conv)
```

```python
import functools
import math

import jax
import jax.numpy as jnp
from jax import lax
from jax.experimental import pallas as pl
from jax.experimental.pallas import tpu as pltpu

F32 = jnp.float32
BF16 = jnp.bfloat16

EPS = 1e-6
ROPE_THETA = 10000.0
CHUNK = 64
DEC_CHUNK = 16

LANES = 128
ROW_TILE = 512
GDN_ROWS = 256
GDN_STACK = 256
VMEM_LIMIT = 56 * 1024 * 1024


def _dot(a, b):
    return jnp.dot(a, b, preferred_element_type=F32)


def _dot_nt(a, b):
    return lax.dot_general(a, b, (((1,), (1,)), ((), ())), preferred_element_type=F32)


def _dot_tn(a, b):
    return lax.dot_general(a, b, (((0,), (0,)), ((), ())), preferred_element_type=F32)


def _rms(x, w):
    return x * lax.rsqrt(jnp.mean(x * x, axis=-1, keepdims=True) + EPS) * w


def _silu(x):
    return x * jax.nn.sigmoid(x)


def _params(n_axes):
    return pltpu.CompilerParams(dimension_semantics=("arbitrary",) * n_axes,
                                vmem_limit_bytes=VMEM_LIMIT)


def _resident(shape, index_map):
    return pl.BlockSpec(shape, index_map, pipeline_mode=pl.Buffered(1))


def _front_kernel(x_ref, cos_ref, sin_ref, prew_ref, win_ref, qn_ref, kvn_ref, wuq_ref, wuqr_ref,
                  wuk_ref, wuvt_ref, q_ref, k_ref, vt_ref, ckv_ref, kpe_ref, qkv_ref, z_ref, ba_ref,
                  *, ql, kvl, qkv_w, z_w, n_heads, q_scale):
    xb = _rms(x_ref[...], prew_ref[...]).astype(BF16)

    def proj(c0, c1):
        return _dot(xb, win_ref[:, c0:c1])

    o = ql + kvl
    cq = proj(0, ql)
    ckv = _rms(proj(ql, o), kvn_ref[...])
    kr = proj(o, o + LANES)
    kr_rot = proj(o + LANES, o + 2 * LANES)
    ba_ref[...] = proj(o + 2 * LANES, o + 3 * LANES)
    o += 3 * LANES
    qkv_ref[...] = proj(o, o + qkv_w)
    z_ref[...] = proj(o + qkv_w, o + qkv_w + z_w)

    ckv_ref[...] = ckv
    cos = cos_ref[...]
    sin = sin_ref[...]
    kpe = kr * cos + kr_rot * sin
    kpe_ref[...] = kpe

    cqn = _rms(cq, qn_ref[...]).astype(BF16)
    ckvb = ckv.astype(BF16)
    q_lin = _dot(cqn, wuq_ref[...])
    q_rot = _dot(cqn, wuqr_ref[...])
    k_nope = _dot(ckvb, wuk_ref[...])
    vt = _dot_nt(wuvt_ref[...], ckvb)
    for h in range(n_heads):
        sl = slice(h * LANES, (h + 1) * LANES)
        q_ref[h] = ((q_lin[:, sl] * cos + q_rot[:, sl] * sin) * q_scale).astype(BF16)
        k_ref[h] = (k_nope[:, sl] + kpe).astype(BF16)
        vt_ref[h] = vt[sl, :].astype(BF16)


def _front(x, cos_t, sin_t, lw, l, dims):
    r, d = x.shape
    tm = ROW_TILE
    nh = dims["n_heads"]
    in_w = lw["w_in"].shape[-1]
    row = lambda w: pl.BlockSpec((tm, w), lambda i: (i, 0))
    wspec = lambda a: _resident((None,) + a.shape[1:], lambda i: (l,) + (0,) * (a.ndim - 1))
    out_shape = (
        jax.ShapeDtypeStruct((nh, r, LANES), BF16),
        jax.ShapeDtypeStruct((nh, r, LANES), BF16),
        jax.ShapeDtypeStruct((nh, LANES, r), BF16),
        jax.ShapeDtypeStruct((r, dims["kvl"]), F32),
        jax.ShapeDtypeStruct((r, LANES), F32),
        jax.ShapeDtypeStruct((r, dims["qkv_w"]), F32),
        jax.ShapeDtypeStruct((r, dims["z_w"]), F32),
        jax.ShapeDtypeStruct((r, LANES), F32),
    )
    out_specs = (
        pl.BlockSpec((nh, tm, LANES), lambda i: (0, i, 0)),
        pl.BlockSpec((nh, tm, LANES), lambda i: (0, i, 0)),
        pl.BlockSpec((nh, LANES, tm), lambda i: (0, 0, i)),
        row(dims["kvl"]), row(LANES), row(dims["qkv_w"]), row(dims["z_w"]), row(LANES),
    )
    weights = [lw["pre_mix_norm"], lw["w_in"], lw["q_norm"], lw["kv_norm"], lw["w_uq"], lw["w_uq_rot"],
               lw["w_uk"], lw["w_uvt"]]
    kern = functools.partial(_front_kernel, ql=dims["ql"], kvl=dims["kvl"], qkv_w=dims["qkv_w"],
                             z_w=dims["z_w"], n_heads=nh, q_scale=dims["q_scale"])
    return pl.pallas_call(
        kern, out_shape=out_shape, grid=(r // tm,),
        in_specs=[row(d), row(LANES), row(LANES)] + [wspec(w) for w in weights],
        out_specs=out_specs, compiler_params=_params(1), name=f"front_l{l}",
    )(x, cos_t, sin_t, *weights)


def _attn_kernel(q_ref, k_ref, vt_ref, km_ref, vmt_ref, o_ref, acc_ref, m_ref, l_ref,
                 *, tile, n_frame_tiles, n_meta):
    i = pl.program_id(1)
    is_frame = i < n_frame_tiles
    halves = []
    for hh in range(2):
        q = q_ref[hh]

        s = _dot_nt(km_ref[hh], q)
        rid = lax.broadcasted_iota(jnp.int32, s.shape, 0)
        s = jnp.where(rid < n_meta, s, -jnp.inf)
        m0 = jnp.max(s, axis=0, keepdims=True)
        p = jnp.exp2(s - m0)
        m_ref[...] = m0
        l_ref[...] = jnp.sum(p, axis=0, keepdims=True)
        acc_ref[...] = _dot(vmt_ref[hh], p.astype(BF16))

        def block(j, masked, hh=hh, q=q):
            off = pl.multiple_of(j * tile, tile)
            s = _dot_nt(k_ref[hh, pl.ds(off, tile), :], q)
            if masked:
                kv_c = lax.broadcasted_iota(jnp.int32, s.shape, 0) // CHUNK
                q_c = lax.broadcasted_iota(jnp.int32, s.shape, 1) // CHUNK
                s = jnp.where(kv_c <= q_c, s, -jnp.inf)
            m_prev = m_ref[...]
            m_new = jnp.maximum(m_prev, jnp.max(s, axis=0, keepdims=True))
            alpha = jnp.exp2(m_prev - m_new)
            p = jnp.exp2(s - m_new)
            l_ref[...] = alpha * l_ref[...] + jnp.sum(p, axis=0, keepdims=True)
            acc_ref[...] = alpha * acc_ref[...] + _dot(vt_ref[hh, :, pl.ds(off, tile)], p.astype(BF16))
            m_ref[...] = m_new

        def body(j, carry):
            block(j, False)
            return carry

        lax.fori_loop(0, jnp.where(is_frame, i, 0), body, 0)

        @pl.when(is_frame)
        def _():
            block(i, True)

        halves.append(acc_ref[...] / l_ref[...])
    o_ref[...] = (halves[0] + halves[1]).T.astype(o_ref.dtype)


def _attention(q, k, vt, dims):
    nh, r, _ = q.shape
    seq = dims["seq"]
    tile = ROW_TILE
    n_ft = seq // tile
    kern = functools.partial(_attn_kernel, tile=tile, n_frame_tiles=n_ft, n_meta=dims["n_meta"])
    return pl.pallas_call(
        kern, out_shape=jax.ShapeDtypeStruct((r, nh * LANES // 2), BF16),
        grid=(nh // 2, n_ft + 1),
        in_specs=[
            pl.BlockSpec((2, tile, LANES), lambda p, i: (p, i, 0)),
            _resident((2, seq, LANES), lambda p, i: (p, 0, 0)),
            _resident((2, LANES, seq), lambda p, i: (p, 0, 0)),
            pl.BlockSpec((2, LANES, LANES), lambda p, i: (p, seq // LANES, 0)),
            pl.BlockSpec((2, LANES, LANES), lambda p, i: (p, 0, seq // LANES)),
        ],
        out_specs=pl.BlockSpec((tile, LANES), lambda p, i: (i, p)),
        scratch_shapes=[pltpu.VMEM((LANES, tile), F32), pltpu.VMEM((1, tile), F32), pltpu.VMEM((1, tile), F32)],
        compiler_params=_params(2), name="prompt_attention",
    )(q, k, vt, k, vt)


def _sattn_kernel(q_ref, ckv_ref, kpe_ref, clat_ref, cpe_ref, wukt_ref, wuv_ref, o_in_ref, o_ref,
                  *, n_heads, nope, rope):
    del o_in_ref
    t = q_ref.shape[1]
    clat = clat_ref[...].astype(BF16)
    cpe = cpe_ref[...].astype(BF16)
    ckv = ckv_ref[...].astype(BF16)
    kpe = kpe_ref[...].astype(BF16)
    qs = [q_ref[h] for h in range(n_heads)]
    q_all = jnp.concatenate(qs, axis=0)
    q_lat = jnp.concatenate([_dot(qs[h][:, :nope], wukt_ref[h]) for h in range(n_heads)],
                            axis=0).astype(BF16)
    q_pe = q_all.astype(F32)[:, nope:nope + rope].astype(BF16)
    s_past = _dot_nt(q_lat, clat) + _dot_nt(q_pe, cpe)
    s_new = _dot_nt(q_lat, ckv) + _dot_nt(q_all, kpe)
    m = jnp.maximum(jnp.max(s_past, axis=-1, keepdims=True), jnp.max(s_new, axis=-1, keepdims=True))
    p_past = jnp.exp2(s_past - m)
    p_new = jnp.exp2(s_new - m)
    den = jnp.sum(p_past, axis=-1, keepdims=True) + jnp.sum(p_new, axis=-1, keepdims=True)
    o_lat = ((_dot(p_past.astype(BF16), clat) + _dot(p_new.astype(BF16), ckv)) / den).astype(BF16)
    o = jnp.concatenate([_dot(o_lat[h * t:(h + 1) * t], wuv_ref[h]) for h in range(n_heads)], axis=1)
    o_ref[...] = o.astype(o_ref.dtype)


def _sample_attention(q, ckv, kpe, cache_lat, cache_pe, lw, l, o_mla, dims):
    nh = dims["n_heads"]
    t = DEC_CHUNK
    nb = cache_lat.shape[1]
    past, kvl = cache_lat.shape[2:]
    rope = cache_pe.shape[-1]
    base = dims["sm0"] // t
    kern = functools.partial(_sattn_kernel, n_heads=nh, nope=dims["nope"], rope=rope)
    return pl.pallas_call(
        kern, out_shape=jax.ShapeDtypeStruct(o_mla.shape, o_mla.dtype), grid=(nb,),
        in_specs=[
            pl.BlockSpec((nh, t, LANES), lambda b: (0, base + b, 0)),
            pl.BlockSpec((t, kvl), lambda b: (base + b, 0)),
            pl.BlockSpec((t, LANES), lambda b: (base + b, 0)),
            pl.BlockSpec((None, None, past, kvl), lambda b: (l, b, 0, 0)),
            pl.BlockSpec((None, None, past, rope), lambda b: (l, b, 0, 0)),
            _resident((None,) + lw["w_uk_t"].shape[1:], lambda b: (l, 0, 0, 0)),
            _resident((None,) + lw["w_uv_h"].shape[1:], lambda b: (l, 0, 0, 0)),
            pl.BlockSpec(memory_space=pl.ANY),
        ],
        out_specs=pl.BlockSpec((t, o_mla.shape[1]), lambda b: (base + b, 0)),
        input_output_aliases={7: 0},
        compiler_params=_params(1), name=f"sample_attention_l{l}",
    )(q, ckv, kpe, cache_lat, cache_pe, lw["w_uk_t"], lw["w_uv_h"], o_mla)


def _split3(x):
    a = x.astype(BF16)
    r1 = x - a.astype(F32)
    b = r1.astype(BF16)
    return a, b, (r1 - b.astype(F32)).astype(BF16)


def _stack_heads(u, base, n_heads):
    return jnp.concatenate([u[:, base + h * LANES: base + (h + 1) * LANES] for h in range(n_heads)], axis=0)


def _gdn_core(u, ba, z, alog, dtb, gnw, states, c):
    rows = u.shape[0]
    nh = len(states[0])
    nblk = rows // c
    dk = LANES
    q_st = _stack_heads(u, 0, nh)
    k_st = _stack_heads(u, nh * dk, nh)
    v_st = _stack_heads(u, 2 * nh * dk, nh)
    qn = q_st * lax.rsqrt(jnp.sum(q_st * q_st, axis=-1, keepdims=True) + EPS) * (dk ** -0.5)
    kn = k_st * lax.rsqrt(jnp.sum(k_st * k_st, axis=-1, keepdims=True) + EPS)

    lane = lax.broadcasted_iota(jnp.int32, ba.shape, 1)
    beta = jax.nn.sigmoid(ba)
    x = ba + dtb
    softplus = jnp.maximum(x, 0.0) + jnp.log1p(jnp.exp(-jnp.abs(x)))
    g = jnp.where((lane >= nh) & (lane < 2 * nh), -jnp.exp(alog) * softplus, 0.0)

    rr = lax.broadcasted_iota(jnp.int32, (rows, rows), 0)
    cc = lax.broadcasted_iota(jnp.int32, (rows, rows), 1)
    tri = jnp.where(((rr // c) == (cc // c)) & (rr >= cc), 1.0, 0.0).astype(BF16)
    g1, g2, g3 = _split3(g)
    gc = _dot(tri, g1) + _dot(tri, g2) + _dot(tri, g3)

    def col(a, lane0):
        return jnp.concatenate([jnp.broadcast_to(a[:, lane0 + h: lane0 + h + 1], (rows, LANES))
                                for h in range(nh)], axis=0)

    b_st = col(beta, 0)
    g_st = col(gc, nh)
    gl_st = jnp.concatenate(
        [jnp.broadcast_to(g_st[h * rows + (s + 1) * c - 1: h * rows + (s + 1) * c], (c, LANES))
         for h in range(nh) for s in range(nblk)], axis=0)
    eg = jnp.exp(g_st)
    vb = v_st * b_st
    kb = kn * b_st * eg
    qg = qn * eg
    kd = kn * jnp.exp(gl_st - g_st)

    n = nh * rows
    r2 = lax.broadcasted_iota(jnp.int32, (n, n), 0)
    c2 = lax.broadcasted_iota(jnp.int32, (n, n), 1)
    same = (r2 // c) == (c2 // c)
    incl = same & (r2 >= c2)
    strict = same & (r2 > c2)
    g_col = jnp.concatenate([g_st] * (n // LANES), axis=1)
    g_row = jnp.broadcast_to(g_st.T[0:1, :], (n, n))
    decay = jnp.exp(jnp.where(incl, g_col - g_row, -jnp.inf))
    knb = kn.astype(BF16)
    kk = _dot_nt(knb, knb)
    qk = _dot_nt(qn.astype(BF16), knb)
    b_col = jnp.concatenate([b_st] * (n // LANES), axis=1)

    a_pow = jnp.where(strict, -(b_col * kk * decay), 0.0)
    t_mat = jnp.where(r2 == c2, 1.0, 0.0) + a_pow
    for _ in range(int(math.log2(c)) - 1):
        ab = a_pow.astype(BF16)
        a_pow = _dot(ab, ab)
        t_mat = t_mat + _dot(t_mat.astype(BF16), a_pow.astype(BF16))

    uw = _dot(t_mat.astype(BF16), jnp.concatenate([vb, kb], axis=1).astype(BF16))
    u_st = uw[:, :LANES]
    w_st = uw[:, LANES:]
    aqk = jnp.where(incl, qk * decay, 0.0).astype(BF16)

    v_new, q_s, new_states = [], [], [[None] * nh for _ in range(nblk)]
    for h in range(nh):
        for s in range(nblk):
            r0 = h * rows + s * c
            st = states[s][h]
            wq = jnp.concatenate([w_st[r0:r0 + c], qg[r0:r0 + c]], axis=0).astype(BF16)
            ws_qs = _dot(wq, st.astype(BF16))
            vn = u_st[r0:r0 + c] - ws_qs[:c]
            v_new.append(vn)
            q_s.append(ws_qs[c:])
            g_last = jnp.exp(g_st[r0 + c - 1: r0 + c])
            new_states[s][h] = st * g_last + _dot_tn(kd[r0:r0 + c].astype(BF16), vn.astype(BF16))
    v_new = jnp.concatenate(v_new, axis=0)
    o_st = jnp.concatenate(q_s, axis=0) + _dot(aqk, v_new.astype(BF16))
    o_n = _rms(o_st, gnw)
    o = jnp.concatenate([o_n[h * rows:(h + 1) * rows] for h in range(nh)], axis=1)
    return o * _silu(z), new_states


def _conv_silu(windows, cw):
    y = windows[0] * cw[0:1]
    for i in range(1, len(windows)):
        y = y + windows[i] * cw[i:i + 1]
    return _silu(y)


def _gdn_frames_kernel(qkv_ref, ba_ref, z_ref, cw_ref, alog_ref, dtb_ref, gnw_ref, hist_ref, s0_ref,
                       o_in_ref, o_ref, s_out_ref, xe_ref, s_ref, *, n_heads, conv_w):
    del o_in_ref
    step = pl.program_id(0)
    rows = qkv_ref.shape[0]

    @pl.when(step == 0)
    def _():
        xe_ref[0:8, :] = hist_ref[...]
        s_ref[...] = s0_ref[...]

    xe_ref[8:8 + rows, :] = qkv_ref[...]
    cw = cw_ref[...]
    off = 8 - (conv_w - 1)
    u = _conv_silu([xe_ref[off + i: off + i + rows, :] for i in range(conv_w)], cw)
    xe_ref[0:8, :] = xe_ref[rows:rows + 8, :]

    states = [[s_ref[h] for h in range(n_heads)]]
    outs = []
    for ck in range(rows // CHUNK):
        sl = slice(ck * CHUNK, (ck + 1) * CHUNK)
        o, states = _gdn_core(u[sl], ba_ref[sl, :], z_ref[sl, :], alog_ref[...], dtb_ref[...], gnw_ref[...],
                              states, CHUNK)
        outs.append(o)
    o_ref[...] = jnp.concatenate(outs, axis=0).astype(o_ref.dtype)
    for h in range(n_heads):
        s_ref[h] = states[0][h]

    @pl.when(step == pl.num_programs(0) - 1)
    def _():
        s_out_ref[...] = s_ref[...]


def _gdn_frames(qkv, ba, z, lw, l, hist8, s0, o_gdn, dims):
    nh = dims["gdn_heads"]
    rows = GDN_ROWS
    wspec = lambda a: _resident((None,) + a.shape[1:], lambda i: (l,) + (0,) * (a.ndim - 1))
    full = lambda a: _resident(a.shape, lambda i: (0,) * a.ndim)
    row = lambda w: pl.BlockSpec((rows, w), lambda i: (i, 0))
    kern = functools.partial(_gdn_frames_kernel, n_heads=nh, conv_w=dims["conv_w"])
    params = [lw["conv_w"], lw["a_log"], lw["dt_bias"], lw["gdn_norm"]]
    return pl.pallas_call(
        kern,
        out_shape=(jax.ShapeDtypeStruct(o_gdn.shape, o_gdn.dtype), jax.ShapeDtypeStruct(s0.shape, F32)),
        grid=(dims["seq"] // rows,),
        in_specs=[row(qkv.shape[1]), row(LANES), row(z.shape[1])] + [wspec(p) for p in params]
                 + [full(hist8), full(s0), pl.BlockSpec(memory_space=pl.ANY)],
        out_specs=(row(o_gdn.shape[1]), pl.BlockSpec(s0.shape, lambda i: (0,) * s0.ndim)),
        scratch_shapes=[pltpu.VMEM((rows + 8, qkv.shape[1]), F32), pltpu.VMEM(s0.shape, F32)],
        input_output_aliases={9: 0},
        compiler_params=_params(1), name=f"gdn_frames_l{l}",
    )(qkv, ba, z, *params, hist8, s0, o_gdn)


def _gdn_small_kernel(xe_ref, ba_ref, z_ref, cw_ref, alog_ref, dtb_ref, gnw_ref, s_in_ref, o_in_ref,
                      o_ref, s_out_ref, *, n_heads, conv_w):
    del o_in_ref
    nseq = xe_ref.shape[0]
    t = DEC_CHUNK
    cw = cw_ref[...]
    u = jnp.concatenate([_conv_silu([xe_ref[s, i:i + t, :] for i in range(conv_w)], cw) for s in range(nseq)],
                        axis=0)
    states = [[s_in_ref[s, h] for h in range(n_heads)] for s in range(nseq)]
    o, states = _gdn_core(u, ba_ref[...], z_ref[...], alog_ref[...], dtb_ref[...], gnw_ref[...], states, t)
    o_ref[...] = o.astype(o_ref.dtype)
    for s in range(nseq):
        for h in range(n_heads):
            s_out_ref[s, h] = states[s][h]


def _gdn_small(xe, ba, z, lw, l, s_in, s_layer, row0, o_gdn, dims):
    nh = dims["gdn_heads"]
    nseq = xe.shape[0]
    sb = CHUNK // DEC_CHUNK
    base = row0 // CHUNK
    wspec = lambda a: _resident((None,) + a.shape[1:], lambda i: (l,) + (0,) * (a.ndim - 1))
    row = lambda w: pl.BlockSpec((CHUNK, w), lambda i: (base + i, 0))
    kern = functools.partial(_gdn_small_kernel, n_heads=nh, conv_w=dims["conv_w"])
    params = [lw["conv_w"], lw["a_log"], lw["dt_bias"], lw["gdn_norm"]]
    st_shape = s_in.shape[1:]
    return pl.pallas_call(
        kern,
        out_shape=(jax.ShapeDtypeStruct(o_gdn.shape, o_gdn.dtype), jax.ShapeDtypeStruct(st_shape, F32)),
        grid=(nseq // sb,),
        in_specs=[pl.BlockSpec((sb,) + xe.shape[1:], lambda i: (i, 0, 0)), row(LANES), row(z.shape[1])]
                 + [wspec(p) for p in params]
                 + [pl.BlockSpec((None, sb) + st_shape[1:], lambda i: (s_layer, i, 0, 0, 0)),
                    pl.BlockSpec(memory_space=pl.ANY)],
        out_specs=(row(o_gdn.shape[1]), pl.BlockSpec((sb,) + st_shape[1:], lambda i: (i, 0, 0, 0))),
        input_output_aliases={8: 0},
        compiler_params=_params(1), name=f"gdn_small_l{l}_r{row0}",
    )(xe, ba, z, *params, s_in, o_gdn)


def _back_kernel(x_ref, om_ref, og_ref, wo_ref, pmn_ref, pfn_ref, wg_ref, wu_ref, wd_ref, pon_ref, y_ref,
                 *, ff_chunks):
    half = om_ref.shape[1]
    mix = _dot(om_ref[...], wo_ref[0:half, :]) + _dot(og_ref[...], wo_ref[half:, :])
    h = x_ref[...] + _rms(mix, pmn_ref[...])
    hn = _rms(h, pfn_ref[...]).astype(BF16)
    f = None
    for c0, c1 in ff_chunks:
        act = (_silu(_dot(hn, wg_ref[:, c0:c1])) * _dot(hn, wu_ref[:, c0:c1])).astype(BF16)
        part = _dot(act, wd_ref[c0:c1, :])
        f = part if f is None else f + part
    y_ref[...] = h + _rms(f, pon_ref[...])


def _back(x, o_mla, o_gdn, lw, l):
    r, d = x.shape
    tm = ROW_TILE
    d_ff = lw["w_gate"].shape[-1]
    n_tiles = d_ff // 256
    cut = 256 * ((n_tiles + 1) // 2)
    ff_chunks = ((0, cut), (cut, d_ff)) if cut < d_ff else ((0, d_ff),)
    row = lambda w: pl.BlockSpec((tm, w), lambda i: (i, 0))
    wspec = lambda a: _resident((None,) + a.shape[1:], lambda i: (l,) + (0,) * (a.ndim - 1))
    weights = [lw["w_o"], lw["post_mix_norm"], lw["pre_ffn_norm"], lw["w_gate"], lw["w_up"], lw["w_down"],
               lw["post_ffn_norm"]]
    return pl.pallas_call(
        functools.partial(_back_kernel, ff_chunks=ff_chunks),
        out_shape=jax.ShapeDtypeStruct((r, d), F32), grid=(r // tm,),
        in_specs=[row(d), row(o_mla.shape[1]), row(o_gdn.shape[1])] + [wspec(w) for w in weights],
        out_specs=row(d), compiler_params=_params(1), name=f"back_l{l}",
    )(x, o_mla, o_gdn, *weights)


def _prepare_weights(pre_mix_norm, w_in, q_norm, kv_norm, w_uq, w_uk, w_uv, conv_w, a_log, dt_bias,
                     gdn_norm, w_o, post_mix_norm, pre_ffn_norm, w_gate, w_up, w_down, post_ffn_norm, dims):
    depth = w_in.shape[0]
    ql, kvl, rope, nope, nh = dims["ql"], dims["kvl"], dims["rope"], dims["nope"], dims["n_heads"]
    gh = dims["gdn_heads"]
    half = rope // 2
    pad = LANES - nope - rope
    zeros = lambda *s: jnp.zeros((depth,) + s, F32)
    d = w_in.shape[1]

    o_pe = ql + kvl
    o_qkv = o_pe + rope
    o_z = o_qkv + dims["qkv_w"]
    o_b = o_z + dims["z_w"]
    kr = w_in[..., o_pe:o_qkv]
    kr_rot = jnp.concatenate([-kr[..., half:], kr[..., :half]], axis=-1)
    seg = lambda a: jnp.concatenate([zeros(d, nope), a, zeros(d, pad)], axis=-1)
    ba_seg = jnp.concatenate([w_in[..., o_b:o_b + 2 * gh], zeros(d, LANES - 2 * gh)], axis=-1)
    w_in_r = jnp.concatenate([w_in[..., :o_pe], seg(kr), seg(kr_rot), ba_seg, w_in[..., o_qkv:o_b]], axis=-1)

    uq = w_uq.reshape(depth, ql, nh, nope + rope)
    uq_n, uq_p = uq[..., :nope], uq[..., nope:]
    hz = lambda w: zeros(ql, nh, w)
    w_uq_p = jnp.concatenate([uq_n, uq_p, hz(pad)], axis=-1).reshape(depth, ql, nh * LANES)
    w_uq_rot = jnp.concatenate([hz(nope), -uq_p[..., half:], uq_p[..., :half], hz(pad)],
                               axis=-1).reshape(depth, ql, nh * LANES)
    w_uk_p = jnp.concatenate([w_uk, zeros(kvl, nh, LANES - nope)], axis=-1).reshape(depth, kvl, nh * LANES)
    vh = w_uv.shape[-1]
    vz = zeros(kvl, nh // 2, 1, vh)
    uv = w_uv.reshape(depth, kvl, nh // 2, 2, vh)
    uv_even = jnp.concatenate([uv[:, :, :, 0:1], vz], axis=-1)
    uv_odd = jnp.concatenate([vz, uv[:, :, :, 1:2]], axis=-1)
    w_uv_p = jnp.concatenate([uv_even, uv_odd], axis=3).reshape(depth, kvl, nh * LANES)

    lane_vec = lambda a: jnp.concatenate([zeros(gh), a, zeros(LANES - 2 * gh)], axis=-1)[:, None, :]
    vec = lambda a: a[:, None, :].astype(F32)
    bf = lambda a: a.astype(BF16)
    return {
        "pre_mix_norm": vec(pre_mix_norm), "w_in": bf(w_in_r), "q_norm": vec(q_norm), "kv_norm": vec(kv_norm),
        "w_uq": bf(w_uq_p), "w_uq_rot": bf(w_uq_rot), "w_uk": bf(w_uk_p),
        "w_uvt": bf(jnp.swapaxes(w_uv_p, 1, 2)),
        "w_uk_t": bf(jnp.transpose(w_uk, (0, 2, 3, 1))), "w_uv_h": bf(jnp.transpose(w_uv, (0, 2, 1, 3))),
        "conv_w": conv_w.astype(F32), "a_log": lane_vec(a_log), "dt_bias": lane_vec(dt_bias),
        "gdn_norm": vec(gdn_norm), "w_o": bf(w_o), "post_mix_norm": vec(post_mix_norm),
        "pre_ffn_norm": vec(pre_ffn_norm), "w_gate": bf(w_gate), "w_up": bf(w_up), "w_down": bf(w_down),
        "post_ffn_norm": vec(post_ffn_norm),
    }


def _rope_tables(pos, dims):
    nope, rope = dims["nope"], dims["rope"]
    half = rope // 2
    inv = ROPE_THETA ** (-jnp.arange(half, dtype=F32) / half)
    ang = pos[:, None] * inv[None, :]
    c, s = jnp.cos(ang), jnp.sin(ang)
    n = pos.shape[0]
    pad = jnp.zeros((n, LANES - nope - rope), F32)
    cos_t = jnp.concatenate([jnp.ones((n, nope), F32), c, c, pad], axis=-1)
    sin_t = jnp.concatenate([jnp.zeros((n, nope), F32), s, s, pad], axis=-1)
    return cos_t, sin_t


def kernel(x_prompt, x_sample, cache_mla_latent, cache_mla_krope, state_gdn, state_gdn_conv, meta_tokens,
           pre_mix_norm, w_in, q_norm, kv_norm, w_uq, w_uk, w_uv, conv_w, a_log, dt_bias, gdn_norm, w_o,
           post_mix_norm, pre_ffn_norm, w_gate, w_up, w_down, post_ffn_norm):
    batch, seq, d = x_prompt.shape
    nb, t, _ = x_sample.shape
    depth, _, past, kvl = cache_mla_latent.shape
    n_meta = meta_tokens.shape[0]
    nh, nope = w_uk.shape[2], w_uk.shape[3]
    rope = cache_mla_krope.shape[-1]
    gh, dk, dv = state_gdn.shape[2:]
    cw = conv_w.shape[1]
    assert batch == 1 and t == DEC_CHUNK and seq % ROW_TILE == 0 and (nb * t) % ROW_TILE == 0
    assert n_meta <= DEC_CHUNK and nh % 2 == 0 and nope + rope <= LANES and 2 * w_uv.shape[3] == LANES
    assert dk == LANES and dv == LANES and gh * CHUNK == GDN_STACK and cw - 1 <= 8 and 2 * gh <= LANES
    dims = dict(seq=seq, n_meta=n_meta, n_heads=nh, nope=nope, rope=rope, ql=q_norm.shape[1], kvl=kvl,
                gdn_heads=gh, conv_w=cw, qkv_w=gh * (2 * dk + dv), z_w=gh * dv,
                q_scale=(nope + rope) ** -0.5 * math.log2(math.e), mt0=seq, sm0=seq + ROW_TILE)
    mt0, sm0 = dims["mt0"], dims["sm0"]
    r = sm0 + nb * t

    lw = _prepare_weights(pre_mix_norm, w_in, q_norm, kv_norm, w_uq, w_uk, w_uv, conv_w, a_log, dt_bias,
                          gdn_norm, w_o, post_mix_norm, pre_ffn_norm, w_gate, w_up, w_down, post_ffn_norm, dims)

    x = jnp.concatenate([x_prompt[0], meta_tokens.astype(F32), jnp.zeros((ROW_TILE - n_meta, d), F32),
                         x_sample.reshape(nb * t, d)], axis=0)
    pos = jnp.concatenate([n_meta + jnp.arange(seq), jnp.arange(n_meta), jnp.zeros((ROW_TILE - n_meta,), jnp.int32),
                           jnp.tile(n_meta + past + jnp.arange(t), nb)]).astype(F32)
    cos_t, sin_t = _rope_tables(pos, dims)

    sb = CHUNK // DEC_CHUNK
    qkv_w = dims["qkv_w"]
    zero_state = jnp.zeros((1, sb, gh, dk, dv), F32)
    outs = [[] for _ in range(8)]
    for l in range(depth):
        q, k, vt, ckv, kpe, qkv, z, ba = _front(x, cos_t, sin_t, lw, l, dims)

        o_mla = _attention(q, k, vt, dims)
        o_mla = _sample_attention(q, ckv, kpe, cache_mla_latent, cache_mla_krope, lw, l, o_mla, dims)

        o_gdn = jnp.zeros((r, dims["z_w"]), BF16)
        xe_meta = jnp.pad(qkv[mt0:mt0 + CHUNK].reshape(sb, DEC_CHUNK, qkv_w), ((0, 0), (cw - 1, 0), (0, 0)))
        o_gdn, s_meta = _gdn_small(xe_meta, ba, z, lw, l, zero_state, 0, mt0, o_gdn, dims)
        hist8 = jnp.pad(qkv[mt0 + n_meta - (cw - 1): mt0 + n_meta], ((8 - (cw - 1), 0), (0, 0)))
        o_gdn, s_prompt = _gdn_frames(qkv, ba, z, lw, l, hist8, s_meta[0], o_gdn, dims)
        qkv_s = qkv[sm0:].reshape(nb, t, qkv_w)
        xe_s = jnp.concatenate([state_gdn_conv[l].astype(F32), qkv_s], axis=1)
        o_gdn, s_sample = _gdn_small(xe_s, ba, z, lw, l, state_gdn, l, sm0, o_gdn, dims)

        x = _back(x, o_mla, o_gdn, lw, l)

        kpe_r = kpe[:, nope:nope + rope]
        order = lambda a: jnp.concatenate([a[mt0:mt0 + n_meta], a[:seq]], axis=0)[None]
        outs[0].append(order(ckv))
        outs[1].append(order(kpe_r))
        outs[2].append(s_prompt[None])
        outs[3].append(qkv[seq - (cw - 1):seq][None])
        outs[4].append(ckv[sm0:].reshape(nb, t, kvl))
        outs[5].append(kpe_r[sm0:].reshape(nb, t, rope))
        outs[6].append(s_sample)
        outs[7].append(xe_s[:, t:])
    y_prompt = x[:seq][None]
    y_sample = x[sm0:].reshape(nb, t, d)
    return (y_prompt, y_sample) + tuple(jnp.stack(o) for o in outs)
```

```python
import functools
import math

import jax
import jax.numpy as jnp
from jax import lax
from jax.experimental import pallas as pl
from jax.experimental.pallas import tpu as pltpu

F32 = jnp.float32
BF16 = jnp.bfloat16

EPS = 1e-6
ROPE_THETA = 10000.0
CHUNK = 64
DEC_CHUNK = 16

LANES = 128
SUBLANES = 8
ROW_TILE = 512
GDN_ROWS = 256
GDN_STACK = 256
VMEM_LIMIT = 56 * 1024 * 1024


def _dot(a, b):
    return jnp.dot(a, b, preferred_element_type=F32)


def _dot_nt(a, b):
    return lax.dot_general(a, b, (((1,), (1,)), ((), ())), preferred_element_type=F32)


def _dot_tn(a, b):
    return lax.dot_general(a, b, (((0,), (0,)), ((), ())), preferred_element_type=F32)


def _rms(x, w):
    return x * lax.rsqrt(jnp.mean(x * x, axis=-1, keepdims=True) + EPS) * w


def _silu(x):
    return x * jax.nn.sigmoid(x)


def _params(n_axes):
    return pltpu.CompilerParams(dimension_semantics=("arbitrary",) * n_axes,
                                vmem_limit_bytes=VMEM_LIMIT)


def _resident(shape, index_map):
    return pl.BlockSpec(shape, index_map, pipeline_mode=pl.Buffered(1))


def _ones_row(h, v_dim):
    return v_dim if h % 2 == 0 else v_dim - 1


def _front_kernel(x_ref, cos_ref, sin_ref, prew_ref, win_ref, qn_ref, kvn_ref, wuq_ref, wuqr_ref,
                  wuk_ref, wuvt_ref, q_ref, k_ref, vt_ref, ckv_ref, kpe_ref, qkv_ref, z_ref, ba_ref,
                  *, ql, kvl, qkv_w, z_w, n_heads, v_dim, q_scale):
    xb = _rms(x_ref[...], prew_ref[...]).astype(BF16)

    def proj(c0, c1):
        return _dot(xb, win_ref[:, c0:c1])

    o = ql + kvl
    cq = proj(0, ql)
    ckv = _rms(proj(ql, o), kvn_ref[...])
    kr = proj(o, o + LANES)
    kr_rot = proj(o + LANES, o + 2 * LANES)
    ba_ref[...] = proj(o + 2 * LANES, o + 3 * LANES)
    o += 3 * LANES
    qkv_ref[...] = proj(o, o + qkv_w)
    z_ref[...] = proj(o + qkv_w, o + qkv_w + z_w)

    ckv_ref[...] = ckv
    cos = cos_ref[...]
    sin = sin_ref[...]
    kpe = kr * cos + kr_rot * sin
    kpe_ref[...] = kpe

    cqn = _rms(cq, qn_ref[...]).astype(BF16)
    ckvb = ckv.astype(BF16)
    q_lin = _dot(cqn, wuq_ref[...])
    q_rot = _dot(cqn, wuqr_ref[...])
    k_nope = _dot(ckvb, wuk_ref[...])
    vt = _dot_nt(wuvt_ref[...], ckvb)
    row = lax.broadcasted_iota(jnp.int32, (LANES, 1), 0)
    for h in range(n_heads):
        sl = slice(h * LANES, (h + 1) * LANES)
        q_ref[h] = ((q_lin[:, sl] * cos + q_rot[:, sl] * sin) * q_scale).astype(BF16)
        k_ref[h] = (k_nope[:, sl] + kpe).astype(BF16)
        vt_ref[h] = (vt[sl, :] + jnp.where(row == _ones_row(h, v_dim), 1.0, 0.0)).astype(BF16)


def _front(x, cos_t, sin_t, lw, l, dims):
    r, d = x.shape
    tm = ROW_TILE
    nh = dims["n_heads"]
    row = lambda w: pl.BlockSpec((tm, w), lambda i: (i, 0))
    wspec = lambda a: _resident((None,) + a.shape[1:], lambda i: (l,) + (0,) * (a.ndim - 1))
    out_shape = (
        jax.ShapeDtypeStruct((nh, r, LANES), BF16),
        jax.ShapeDtypeStruct((nh, r, LANES), BF16),
        jax.ShapeDtypeStruct((nh, LANES, r), BF16),
        jax.ShapeDtypeStruct((r, dims["kvl"]), F32),
        jax.ShapeDtypeStruct((r, LANES), F32),
        jax.ShapeDtypeStruct((r, dims["qkv_w"]), F32),
        jax.ShapeDtypeStruct((r, dims["z_w"]), F32),
        jax.ShapeDtypeStruct((r, LANES), F32),
    )
    out_specs = (
        pl.BlockSpec((nh, tm, LANES), lambda i: (0, i, 0)),
        pl.BlockSpec((nh, tm, LANES), lambda i: (0, i, 0)),
        pl.BlockSpec((nh, LANES, tm), lambda i: (0, 0, i)),
        row(dims["kvl"]), row(LANES), row(dims["qkv_w"]), row(dims["z_w"]), row(LANES),
    )
    weights = [lw["pre_mix_norm"], lw["w_in"], lw["q_norm"], lw["kv_norm"], lw["w_uq"], lw["w_uq_rot"],
               lw["w_uk"], lw["w_uvt"]]
    kern = functools.partial(_front_kernel, ql=dims["ql"], kvl=dims["kvl"], qkv_w=dims["qkv_w"],
                             z_w=dims["z_w"], n_heads=nh, v_dim=dims["v_dim"], q_scale=dims["q_scale"])
    return pl.pallas_call(
        kern, out_shape=out_shape, grid=(r // tm,),
        in_specs=[row(d), row(LANES), row(LANES)] + [wspec(w) for w in weights],
        out_specs=out_specs, compiler_params=_params(1), name=f"front_l{l}",
    )(x, cos_t, sin_t, *weights)


def _attn_kernel(q_ref, k_ref, vt_ref, o_ref, acc_ref, m_ref, *, tile, fr0, n_meta, v_dim):
    sup = pl.program_id(1)
    is_frame = sup >= 1
    streams = [(hh, c) for hh in range(2) for c in range(2)]
    q = {(hh, c): q_ref[hh, c * tile:(c + 1) * tile, :] for hh, c in streams}

    def scores(hh, c, off):
        return _dot_nt(k_ref[hh, pl.ds(off, tile), :], q[hh, c])

    def update(hh, c, s, off, masked):
        if masked:
            kv_c = lax.broadcasted_iota(jnp.int32, s.shape, 0) // CHUNK
            q_c = lax.broadcasted_iota(jnp.int32, s.shape, 1) // CHUNK
            s = jnp.where(kv_c <= q_c, s, -jnp.inf)
        m_prev = m_ref[hh, c]
        m_new = jnp.maximum(m_prev, jnp.max(s, axis=0, keepdims=True))
        alpha = jnp.exp2(m_prev - m_new)
        p = jnp.exp2(s - m_new).astype(BF16)
        acc_ref[hh, c] = alpha * acc_ref[hh, c] + _dot(vt_ref[hh, :, pl.ds(off, tile)], p)
        m_ref[hh, c] = m_new

    mb = LANES
    for hh, c in streams:
        s = _dot_nt(k_ref[hh, fr0 - mb:fr0, :], q[hh, c])
        rid = lax.broadcasted_iota(jnp.int32, s.shape, 0)
        s = jnp.where(rid >= mb - n_meta, s, -jnp.inf)
        m0 = jnp.max(s, axis=0, keepdims=True)
        m_ref[hh, c] = m0
        acc_ref[hh, c] = _dot(vt_ref[hh, :, fr0 - mb:fr0], jnp.exp2(s - m0).astype(BF16))

    def body(j, carry):
        off = pl.multiple_of(fr0 + j * tile, tile)
        ss = [scores(hh, c, off) for hh, c in streams]
        for (hh, c), s in zip(streams, ss):
            update(hh, c, s, off, False)
        return carry

    first = 2 * (sup - 1)
    lax.fori_loop(0, jnp.where(is_frame, first, 0), body, 0)

    @pl.when(is_frame)
    def _():
        off_a = pl.multiple_of(fr0 + first * tile, tile)
        off_b = pl.multiple_of(fr0 + (first + 1) * tile, tile)
        work = [(hh, c, off, masked) for hh in range(2)
                for c, off, masked in ((0, off_a, True), (1, off_a, False), (1, off_b, True))]
        ss = [scores(hh, c, off) for hh, c, off, _ in work]
        for (hh, c, off, masked), s in zip(work, ss):
            update(hh, c, s, off, masked)

    rows = lax.broadcasted_iota(jnp.int32, (LANES, tile), 0)
    r_even, r_odd = _ones_row(0, v_dim), _ones_row(1, v_dim)
    for c in range(2):
        o = jnp.where(rows < v_dim, acc_ref[0, c] / acc_ref[0, c, r_even:r_even + 1, :],
                      acc_ref[1, c] / acc_ref[1, c, r_odd:r_odd + 1, :])
        o_ref[c * tile:(c + 1) * tile, :] = o.T.astype(o_ref.dtype)


def _attention(q, k, vt, dims):
    nh, r, _ = q.shape
    tile = ROW_TILE
    kern = functools.partial(_attn_kernel, tile=tile, fr0=dims["fr0"], n_meta=dims["n_meta"],
                             v_dim=dims["v_dim"])
    return pl.pallas_call(
        kern, out_shape=jax.ShapeDtypeStruct((r, nh * LANES // 2), BF16),
        grid=(nh // 2, r // (2 * tile)),
        in_specs=[
            pl.BlockSpec((2, 2 * tile, LANES), lambda p, i: (p, i, 0)),
            _resident((2, r, LANES), lambda p, i: (p, 0, 0)),
            _resident((2, LANES, r), lambda p, i: (p, 0, 0)),
        ],
        out_specs=pl.BlockSpec((2 * tile, LANES), lambda p, i: (i, p)),
        scratch_shapes=[pltpu.VMEM((2, 2, LANES, tile), F32), pltpu.VMEM((2, 2, 1, tile), F32)],
        compiler_params=_params(2), name="prompt_attention",
    )(q, k, vt)


def _sattn_kernel(q_ref, ckv_ref, kpe_ref, clat_ref, cpe_ref, wukt_ref, wuv_ref, o_in_ref, o_ref,
                  *, n_heads, nope, rope):
    del o_in_ref
    t = q_ref.shape[1]
    clat = clat_ref[...].astype(BF16)
    cpe = cpe_ref[...].astype(BF16)
    ckv = ckv_ref[...].astype(BF16)
    kpe = kpe_ref[...].astype(BF16)
    qs = [q_ref[h] for h in range(n_heads)]
    q_all = jnp.concatenate(qs, axis=0)
    q_lat = jnp.concatenate([_dot(qs[h][:, :nope], wukt_ref[h]) for h in range(n_heads)],
                            axis=0).astype(BF16)
    q_pe = q_all.astype(F32)[:, nope:nope + rope].astype(BF16)
    s_past = _dot_nt(q_lat, clat) + _dot_nt(q_pe, cpe)
    s_new = _dot_nt(q_lat, ckv) + _dot_nt(q_all, kpe)
    m = jnp.maximum(jnp.max(s_past, axis=-1, keepdims=True), jnp.max(s_new, axis=-1, keepdims=True))
    p_past = jnp.exp2(s_past - m)
    p_new = jnp.exp2(s_new - m)
    den = jnp.sum(p_past, axis=-1, keepdims=True) + jnp.sum(p_new, axis=-1, keepdims=True)
    o_lat = ((_dot(p_past.astype(BF16), clat) + _dot(p_new.astype(BF16), ckv)) / den).astype(BF16)
    o = jnp.concatenate([_dot(o_lat[h * t:(h + 1) * t], wuv_ref[h]) for h in range(n_heads)], axis=1)
    o_ref[...] = o.astype(o_ref.dtype)


def _sample_attention(q, ckv, kpe, cache_lat, cache_pe, lw, l, o_mla, dims):
    nh = dims["n_heads"]
    t = DEC_CHUNK
    nb = cache_lat.shape[1]
    past, kvl = cache_lat.shape[2:]
    rope = cache_pe.shape[-1]
    kern = functools.partial(_sattn_kernel, n_heads=nh, nope=dims["nope"], rope=rope)
    return pl.pallas_call(
        kern, out_shape=jax.ShapeDtypeStruct(o_mla.shape, o_mla.dtype), grid=(nb,),
        in_specs=[
            pl.BlockSpec((nh, t, LANES), lambda b: (0, b, 0)),
            pl.BlockSpec((t, kvl), lambda b: (b, 0)),
            pl.BlockSpec((t, LANES), lambda b: (b, 0)),
            pl.BlockSpec((None, None, past, kvl), lambda b: (l, b, 0, 0)),
            pl.BlockSpec((None, None, past, rope), lambda b: (l, b, 0, 0)),
            _resident((None,) + lw["w_uk_t"].shape[1:], lambda b: (l, 0, 0, 0)),
            _resident((None,) + lw["w_uv_h"].shape[1:], lambda b: (l, 0, 0, 0)),
            pl.BlockSpec(memory_space=pl.ANY),
        ],
        out_specs=pl.BlockSpec((t, o_mla.shape[1]), lambda b: (b, 0)),
        input_output_aliases={7: 0},
        compiler_params=_params(1), name=f"sample_attention_l{l}",
    )(q, ckv, kpe, cache_lat, cache_pe, lw["w_uk_t"], lw["w_uv_h"], o_mla)


def _split3(x):
    a = x.astype(BF16)
    r1 = x - a.astype(F32)
    b = r1.astype(BF16)
    return a, b, (r1 - b.astype(F32)).astype(BF16)


def _stack_heads(u, base, n_heads):
    return jnp.concatenate([u[:, base + h * LANES: base + (h + 1) * LANES] for h in range(n_heads)], axis=0)


def _gdn_prep(u, ba, alog, dtb, nh, c, consts):
    rows = u.shape[0]
    nblk = rows // c
    dk = LANES
    tri, incl, strict, eye = consts
    q_st = _stack_heads(u, 0, nh)
    k_st = _stack_heads(u, nh * dk, nh)
    v_st = _stack_heads(u, 2 * nh * dk, nh)
    qn = q_st * lax.rsqrt(jnp.sum(q_st * q_st, axis=-1, keepdims=True) + EPS) * (dk ** -0.5)
    kn = k_st * lax.rsqrt(jnp.sum(k_st * k_st, axis=-1, keepdims=True) + EPS)

    lane = lax.broadcasted_iota(jnp.int32, ba.shape, 1)
    beta = jax.nn.sigmoid(ba)
    x = ba + dtb
    softplus = jnp.maximum(x, 0.0) + jnp.log1p(jnp.exp(-jnp.abs(x)))
    g = jnp.where((lane >= nh) & (lane < 2 * nh), -jnp.exp(alog) * softplus, 0.0)
    g1, g2, g3 = _split3(g)
    gc = _dot(tri, g1) + _dot(tri, g2) + _dot(tri, g3)

    def col(a, lane0):
        return jnp.concatenate([jnp.broadcast_to(a[:, lane0 + h: lane0 + h + 1], (rows, LANES))
                                for h in range(nh)], axis=0)

    b_st = col(beta, 0)
    g_st = col(gc, nh)
    gl_st = jnp.concatenate(
        [jnp.broadcast_to(g_st[h * rows + (s + 1) * c - 1: h * rows + (s + 1) * c], (c, LANES))
         for h in range(nh) for s in range(nblk)], axis=0)
    eg = jnp.exp(g_st)
    n = nh * rows
    g_col = jnp.concatenate([g_st] * (n // LANES), axis=1)
    g_row = jnp.broadcast_to(g_st.T[0:1, :], (n, n))
    decay = jnp.exp(jnp.where(incl, g_col - g_row, -jnp.inf))
    knb = kn.astype(BF16)
    kk = _dot_nt(knb, knb)
    qk = _dot_nt(qn.astype(BF16), knb)
    b_col = jnp.concatenate([b_st] * (n // LANES), axis=1)
    a0 = jnp.where(strict, -(b_col * kk * decay), 0.0)
    return dict(
        a=a0, t=eye + a0,
        vk=jnp.concatenate([v_st * b_st, kn * b_st * eg], axis=1).astype(BF16),
        aqk=(qk * decay).astype(BF16), qg=(qn * eg).astype(BF16),
        kd=(kn * jnp.exp(gl_st - g_st)).astype(BF16), g_st=g_st)


def _gdn_chunks(us, bas, zs, alog, dtb, gnw, states, c):
    rows = us[0].shape[0]
    nh = len(states[0])
    nblk = rows // c
    n = nh * rows
    rr = lax.broadcasted_iota(jnp.int32, (rows, rows), 0)
    cc = lax.broadcasted_iota(jnp.int32, (rows, rows), 1)
    tri = jnp.where(((rr // c) == (cc // c)) & (rr >= cc), 1.0, 0.0).astype(BF16)
    r2 = lax.broadcasted_iota(jnp.int32, (n, n), 0)
    c2 = lax.broadcasted_iota(jnp.int32, (n, n), 1)
    same = (r2 // c) == (c2 // c)
    consts = (tri, same & (r2 >= c2), same & (r2 > c2), jnp.where(r2 == c2, 1.0, 0.0))

    pre = [_gdn_prep(u, ba, alog, dtb, nh, c, consts) for u, ba in zip(us, bas)]

    a_bf = [p["a"].astype(BF16) for p in pre]
    t_mat = [p["t"] for p in pre]
    for _ in range(int(math.log2(c)) - 1):
        a_bf = [_dot(ab, ab).astype(BF16) for ab in a_bf]
        t_mat = [t + _dot(t.astype(BF16), ab) for t, ab in zip(t_mat, a_bf)]
    uw = [_dot(t.astype(BF16), p["vk"]) for t, p in zip(t_mat, pre)]

    outs = []
    for ci, p in enumerate(pre):
        u_st, w_st = uw[ci][:, :LANES], uw[ci][:, LANES:]
        v_new, q_s, new_states = [], [], [[None] * nh for _ in range(nblk)]
        for h in range(nh):
            for s in range(nblk):
                r0 = h * rows + s * c
                st = states[s][h]
                wq = jnp.concatenate([w_st[r0:r0 + c].astype(BF16), p["qg"][r0:r0 + c]], axis=0)
                ws_qs = _dot(wq, st.astype(BF16))
                vn = u_st[r0:r0 + c] - ws_qs[:c]
                v_new.append(vn)
                q_s.append(ws_qs[c:])
                g_last = jnp.exp(p["g_st"][r0 + c - 1: r0 + c])
                new_states[s][h] = st * g_last + _dot_tn(p["kd"][r0:r0 + c], vn.astype(BF16))
        states = new_states
        v_new = jnp.concatenate(v_new, axis=0)
        o_st = jnp.concatenate(q_s, axis=0) + _dot(p["aqk"], v_new.astype(BF16))
        o_n = _rms(o_st, gnw)
        o = jnp.concatenate([o_n[h * rows:(h + 1) * rows] for h in range(nh)], axis=1)
        outs.append(o * _silu(zs[ci]))
    return outs, states


def _conv_silu(windows, cw):
    y = windows[0] * cw[0:1]
    for i in range(1, len(windows)):
        y = y + windows[i] * cw[i:i + 1]
    return _silu(y)


def _gdn_prompt_kernel(qkv_ref, ba_ref, z_ref, cw_ref, alog_ref, dtb_ref, gnw_ref, o_ref, s_out_ref,
                       xe_ref, s_ref, *, n_heads, conv_w, lead_rows):
    step = pl.program_id(0)
    rows = qkv_ref.shape[0]

    @pl.when(step == 0)
    def _():
        xe_ref[0:SUBLANES, :] = jnp.zeros((SUBLANES, xe_ref.shape[1]), F32)
        s_ref[...] = jnp.zeros(s_ref.shape, F32)

    seq_row = step * rows + lax.broadcasted_iota(jnp.int32, (rows, 1), 0)
    xe_ref[SUBLANES:SUBLANES + rows, :] = jnp.where(seq_row >= lead_rows, qkv_ref[...], 0.0)
    off = SUBLANES - (conv_w - 1)
    u = _conv_silu([xe_ref[off + i: off + i + rows, :] for i in range(conv_w)], cw_ref[...])
    xe_ref[0:SUBLANES, :] = xe_ref[rows:rows + SUBLANES, :]

    chunks = [slice(ck * CHUNK, (ck + 1) * CHUNK) for ck in range(rows // CHUNK)]
    outs, states = _gdn_chunks([u[sl] for sl in chunks], [ba_ref[sl, :] for sl in chunks],
                               [z_ref[sl, :] for sl in chunks], alog_ref[...], dtb_ref[...], gnw_ref[...],
                               [[s_ref[h] for h in range(n_heads)]], CHUNK)
    o_ref[...] = jnp.concatenate(outs, axis=0).astype(o_ref.dtype)
    for h in range(n_heads):
        s_ref[h] = states[0][h]

    @pl.when(step == pl.num_programs(0) - 1)
    def _():
        s_out_ref[...] = s_ref[...]


def _gdn_prompt(qkv, ba, z, lw, l, dims):
    nh = dims["gdn_heads"]
    rows = GDN_ROWS
    r = qkv.shape[0]
    base = dims["sm"] // rows
    st_shape = (nh, LANES, LANES)
    wspec = lambda a: _resident((None,) + a.shape[1:], lambda i: (l,) + (0,) * (a.ndim - 1))
    row = lambda w: pl.BlockSpec((rows, w), lambda i: (base + i, 0))
    kern = functools.partial(_gdn_prompt_kernel, n_heads=nh, conv_w=dims["conv_w"],
                             lead_rows=ROW_TILE - dims["n_meta"])
    params = [lw["conv_w"], lw["a_log"], lw["dt_bias"], lw["gdn_norm"]]
    return pl.pallas_call(
        kern,
        out_shape=(jax.ShapeDtypeStruct((r, z.shape[1]), BF16), jax.ShapeDtypeStruct(st_shape, F32)),
        grid=((r - dims["sm"]) // rows,),
        in_specs=[row(qkv.shape[1]), row(LANES), row(z.shape[1])] + [wspec(p) for p in params],
        out_specs=(row(z.shape[1]), pl.BlockSpec(st_shape, lambda i: (0, 0, 0))),
        scratch_shapes=[pltpu.VMEM((rows + SUBLANES, qkv.shape[1]), F32), pltpu.VMEM(st_shape, F32)],
        compiler_params=_params(1), name=f"gdn_prompt_l{l}",
    )(qkv, ba, z, *params)


def _gdn_sample_kernel(xe_ref, ba_ref, z_ref, cw_ref, alog_ref, dtb_ref, gnw_ref, s_in_ref, o_in_ref,
                       o_ref, s_out_ref, *, n_heads, conv_w):
    del o_in_ref
    nseq = xe_ref.shape[0]
    t = DEC_CHUNK
    cw = cw_ref[...]
    u = jnp.concatenate([_conv_silu([xe_ref[s, i:i + t, :] for i in range(conv_w)], cw) for s in range(nseq)],
                        axis=0)
    states = [[s_in_ref[s, h] for h in range(n_heads)] for s in range(nseq)]
    outs, states = _gdn_chunks([u], [ba_ref[...]], [z_ref[...]], alog_ref[...], dtb_ref[...], gnw_ref[...],
                               states, t)
    o_ref[...] = outs[0].astype(o_ref.dtype)
    for s in range(nseq):
        for h in range(n_heads):
            s_out_ref[s, h] = states[s][h]


def _gdn_sample(xe, ba, z, lw, l, s_in, o_gdn, dims):
    nh = dims["gdn_heads"]
    nseq = xe.shape[0]
    sb = CHUNK // DEC_CHUNK
    wspec = lambda a: _resident((None,) + a.shape[1:], lambda i: (l,) + (0,) * (a.ndim - 1))
    row = lambda w: pl.BlockSpec((CHUNK, w), lambda i: (i, 0))
    kern = functools.partial(_gdn_sample_kernel, n_heads=nh, conv_w=dims["conv_w"])
    params = [lw["conv_w"], lw["a_log"], lw["dt_bias"], lw["gdn_norm"]]
    st_shape = s_in.shape[1:]
    return pl.pallas_call(
        kern,
        out_shape=(jax.ShapeDtypeStruct(o_gdn.shape, o_gdn.dtype), jax.ShapeDtypeStruct(st_shape, F32)),
        grid=(nseq // sb,),
        in_specs=[pl.BlockSpec((sb,) + xe.shape[1:], lambda i: (i, 0, 0)), row(LANES), row(z.shape[1])]
                 + [wspec(p) for p in params]
                 + [pl.BlockSpec((None, sb) + st_shape[1:], lambda i: (l, i, 0, 0, 0)),
                    pl.BlockSpec(memory_space=pl.ANY)],
        out_specs=(row(o_gdn.shape[1]), pl.BlockSpec((sb,) + st_shape[1:], lambda i: (i, 0, 0, 0))),
        input_output_aliases={8: 0},
        compiler_params=_params(1), name=f"gdn_sample_l{l}",
    )(xe, ba, z, *params, s_in, o_gdn)


def _back_kernel(x_ref, om_ref, og_ref, wo_ref, pmn_ref, pfn_ref, wg_ref, wu_ref, wd_ref, pon_ref, y_ref,
                 *, ff_chunks):
    half = om_ref.shape[1]
    mix = _dot(om_ref[...], wo_ref[0:half, :]) + _dot(og_ref[...], wo_ref[half:, :])
    h = x_ref[...] + _rms(mix, pmn_ref[...])
    hn = _rms(h, pfn_ref[...]).astype(BF16)
    f = None
    for c0, c1 in ff_chunks:
        act = (_silu(_dot(hn, wg_ref[:, c0:c1])) * _dot(hn, wu_ref[:, c0:c1])).astype(BF16)
        part = _dot(act, wd_ref[c0:c1, :])
        f = part if f is None else f + part
    y_ref[...] = h + _rms(f, pon_ref[...])


def _back(x, o_mla, o_gdn, lw, l):
    r, d = x.shape
    tm = ROW_TILE
    d_ff = lw["w_gate"].shape[-1]
    n_tiles = d_ff // 256
    cut = 256 * ((n_tiles + 1) // 2)
    ff_chunks = ((0, cut), (cut, d_ff)) if cut < d_ff else ((0, d_ff),)
    row = lambda w: pl.BlockSpec((tm, w), lambda i: (i, 0))
    wspec = lambda a: _resident((None,) + a.shape[1:], lambda i: (l,) + (0,) * (a.ndim - 1))
    weights = [lw["w_o"], lw["post_mix_norm"], lw["pre_ffn_norm"], lw["w_gate"], lw["w_up"], lw["w_down"],
               lw["post_ffn_norm"]]
    return pl.pallas_call(
        functools.partial(_back_kernel, ff_chunks=ff_chunks),
        out_shape=jax.ShapeDtypeStruct((r, d), F32), grid=(r // tm,),
        in_specs=[row(d), row(o_mla.shape[1]), row(o_gdn.shape[1])] + [wspec(w) for w in weights],
        out_specs=row(d), compiler_params=_params(1), name=f"back_l{l}",
    )(x, o_mla, o_gdn, *weights)


def _prepare_weights(pre_mix_norm, w_in, q_norm, kv_norm, w_uq, w_uk, w_uv, conv_w, a_log, dt_bias,
                     gdn_norm, w_o, post_mix_norm, pre_ffn_norm, w_gate, w_up, w_down, post_ffn_norm, dims):
    depth = w_in.shape[0]
    ql, kvl, rope, nope, nh = dims["ql"], dims["kvl"], dims["rope"], dims["nope"], dims["n_heads"]
    gh = dims["gdn_heads"]
    half = rope // 2
    pad = LANES - nope - rope
    zeros = lambda *s: jnp.zeros((depth,) + s, F32)
    d = w_in.shape[1]

    o_pe = ql + kvl
    o_qkv = o_pe + rope
    o_z = o_qkv + dims["qkv_w"]
    o_b = o_z + dims["z_w"]
    kr = w_in[..., o_pe:o_qkv]
    kr_rot = jnp.concatenate([-kr[..., half:], kr[..., :half]], axis=-1)
    seg = lambda a: jnp.concatenate([zeros(d, nope), a, zeros(d, pad)], axis=-1)
    ba_seg = jnp.concatenate([w_in[..., o_b:o_b + 2 * gh], zeros(d, LANES - 2 * gh)], axis=-1)
    w_in_r = jnp.concatenate([w_in[..., :o_pe], seg(kr), seg(kr_rot), ba_seg, w_in[..., o_qkv:o_b]], axis=-1)

    uq = w_uq.reshape(depth, ql, nh, nope + rope)
    uq_n, uq_p = uq[..., :nope], uq[..., nope:]
    hz = lambda w: zeros(ql, nh, w)
    w_uq_p = jnp.concatenate([uq_n, uq_p, hz(pad)], axis=-1).reshape(depth, ql, nh * LANES)
    w_uq_rot = jnp.concatenate([hz(nope), -uq_p[..., half:], uq_p[..., :half], hz(pad)],
                               axis=-1).reshape(depth, ql, nh * LANES)
    w_uk_p = jnp.concatenate([w_uk, zeros(kvl, nh, LANES - nope)], axis=-1).reshape(depth, kvl, nh * LANES)
    vh = w_uv.shape[-1]
    vz = zeros(kvl, nh // 2, 1, vh)
    uv = w_uv.reshape(depth, kvl, nh // 2, 2, vh)
    uv_even = jnp.concatenate([uv[:, :, :, 0:1], vz], axis=-1)
    uv_odd = jnp.concatenate([vz, uv[:, :, :, 1:2]], axis=-1)
    w_uv_p = jnp.concatenate([uv_even, uv_odd], axis=3).reshape(depth, kvl, nh * LANES)

    lane_vec = lambda a: jnp.concatenate([zeros(gh), a, zeros(LANES - 2 * gh)], axis=-1)[:, None, :]
    vec = lambda a: a[:, None, :].astype(F32)
    bf = lambda a: a.astype(BF16)
    return {
        "pre_mix_norm": vec(pre_mix_norm), "w_in": bf(w_in_r), "q_norm": vec(q_norm), "kv_norm": vec(kv_norm),
        "w_uq": bf(w_uq_p), "w_uq_rot": bf(w_uq_rot), "w_uk": bf(w_uk_p),
        "w_uvt": bf(jnp.swapaxes(w_uv_p, 1, 2)),
        "w_uk_t": bf(jnp.transpose(w_uk, (0, 2, 3, 1))), "w_uv_h": bf(jnp.transpose(w_uv, (0, 2, 1, 3))),
        "conv_w": conv_w.astype(F32), "a_log": lane_vec(a_log), "dt_bias": lane_vec(dt_bias),
        "gdn_norm": vec(gdn_norm), "w_o": bf(w_o), "post_mix_norm": vec(post_mix_norm),
        "pre_ffn_norm": vec(pre_ffn_norm), "w_gate": bf(w_gate), "w_up": bf(w_up), "w_down": bf(w_down),
        "post_ffn_norm": vec(post_ffn_norm),
    }


def _rope_tables(pos, dims):
    nope, rope = dims["nope"], dims["rope"]
    half = rope // 2
    inv = ROPE_THETA ** (-jnp.arange(half, dtype=F32) / half)
    ang = pos[:, None] * inv[None, :]
    c, s = jnp.cos(ang), jnp.sin(ang)
    n = pos.shape[0]
    pad = jnp.zeros((n, LANES - nope - rope), F32)
    cos_t = jnp.concatenate([jnp.ones((n, nope), F32), c, c, pad], axis=-1)
    sin_t = jnp.concatenate([jnp.zeros((n, nope), F32), s, s, pad], axis=-1)
    return cos_t, sin_t


def kernel(x_prompt, x_sample, cache_mla_latent, cache_mla_krope, state_gdn, state_gdn_conv, meta_tokens,
           pre_mix_norm, w_in, q_norm, kv_norm, w_uq, w_uk, w_uv, conv_w, a_log, dt_bias, gdn_norm, w_o,
           post_mix_norm, pre_ffn_norm, w_gate, w_up, w_down, post_ffn_norm):
    batch, seq, d = x_prompt.shape
    nb, t, _ = x_sample.shape
    depth, _, past, kvl = cache_mla_latent.shape
    n_meta = meta_tokens.shape[0]
    nh, nope = w_uk.shape[2], w_uk.shape[3]
    rope = cache_mla_krope.shape[-1]
    gh, dk, dv = state_gdn.shape[2:]
    cw = conv_w.shape[1]
    sm = nb * t
    fr0 = sm + ROW_TILE
    mt0 = fr0 - n_meta
    r = fr0 + seq
    assert batch == 1 and t == DEC_CHUNK and seq % (2 * ROW_TILE) == 0 and fr0 % (2 * ROW_TILE) == 0
    assert n_meta <= CHUNK and nh % 2 == 0 and nope + rope <= LANES and 2 * w_uv.shape[3] == LANES
    assert dk == LANES and dv == LANES and gh * CHUNK == GDN_STACK and cw - 1 <= SUBLANES and 2 * gh <= LANES
    assert sm % GDN_ROWS == 0 and (r - sm) % GDN_ROWS == 0 and n_meta + cw - 1 <= ROW_TILE
    dims = dict(n_meta=n_meta, n_heads=nh, nope=nope, rope=rope, v_dim=w_uv.shape[3], ql=q_norm.shape[1],
                kvl=kvl, gdn_heads=gh, conv_w=cw, qkv_w=gh * (2 * dk + dv), z_w=gh * dv,
                q_scale=(nope + rope) ** -0.5 * math.log2(math.e), sm=sm, fr0=fr0)

    lw = _prepare_weights(pre_mix_norm, w_in, q_norm, kv_norm, w_uq, w_uk, w_uv, conv_w, a_log, dt_bias,
                          gdn_norm, w_o, post_mix_norm, pre_ffn_norm, w_gate, w_up, w_down, post_ffn_norm, dims)

    x = jnp.concatenate([x_sample.reshape(sm, d), jnp.zeros((ROW_TILE - n_meta, d), F32),
                         meta_tokens.astype(F32), x_prompt[0]], axis=0)
    pos = jnp.concatenate([jnp.tile(n_meta + past + jnp.arange(t), nb), jnp.zeros((ROW_TILE - n_meta,), jnp.int32),
                           jnp.arange(n_meta + seq)]).astype(F32)
    cos_t, sin_t = _rope_tables(pos, dims)

    qkv_w = dims["qkv_w"]
    outs = [[] for _ in range(8)]
    for l in range(depth):
        q, k, vt, ckv, kpe, qkv, z, ba = _front(x, cos_t, sin_t, lw, l, dims)

        o_mla = _attention(q, k, vt, dims)
        o_mla = _sample_attention(q, ckv, kpe, cache_mla_latent, cache_mla_krope, lw, l, o_mla, dims)

        o_gdn, s_prompt = _gdn_prompt(qkv, ba, z, lw, l, dims)
        xe_s = jnp.concatenate([state_gdn_conv[l].astype(F32), qkv[:sm].reshape(nb, t, qkv_w)], axis=1)
        o_gdn, s_sample = _gdn_sample(xe_s, ba, z, lw, l, state_gdn, o_gdn, dims)

        x = _back(x, o_mla, o_gdn, lw, l)

        kpe_r = kpe[:, nope:nope + rope]
        outs[0].append(ckv[mt0:][None])
        outs[1].append(kpe_r[mt0:][None])
        outs[2].append(s_prompt[None])
        outs[3].append(qkv[r - (cw - 1):][None])
        outs[4].append(ckv[:sm].reshape(nb, t, kvl))
        outs[5].append(kpe_r[:sm].reshape(nb, t, rope))
        outs[6].append(s_sample)
        outs[7].append(xe_s[:, t:])
    y_prompt = x[fr0:][None]
    y_sample = x[:sm].reshape(nb, t, d)
    return (y_prompt, y_sample) + tuple(jnp.stack(o) for o in outs)
```

```python
import functools
import math

import jax
import jax.numpy as jnp
from jax import lax
from jax.experimental import pallas as pl
from jax.experimental.pallas import tpu as pltpu

F32 = jnp.float32
BF16 = jnp.bfloat16

EPS = 1e-6
ROPE_THETA = 10000.0
CHUNK = 64
DEC_CHUNK = 16

LANES = 128
SUBLANES = 8
ROW_TILE = 512
GDN_ROWS = 512
GDN_GROUP = 2
GDN_STACK = 256
VMEM_LIMIT = 56 * 1024 * 1024


def _dot(a, b):
    return jnp.dot(a, b, preferred_element_type=F32)


def _dot_nt(a, b):
    return lax.dot_general(a, b, (((1,), (1,)), ((), ())), preferred_element_type=F32)


def _dot_tn(a, b):
    return lax.dot_general(a, b, (((0,), (0,)), ((), ())), preferred_element_type=F32)


def _rms(x, w):
    return x * lax.rsqrt(jnp.mean(x * x, axis=-1, keepdims=True) + EPS) * w


def _silu(x):
    return x * jax.nn.sigmoid(x)


def _params(n_axes):
    return pltpu.CompilerParams(dimension_semantics=("arbitrary",) * n_axes,
                                vmem_limit_bytes=VMEM_LIMIT)


def _resident(shape, index_map):
    return pl.BlockSpec(shape, index_map, pipeline_mode=pl.Buffered(1))


def _vt_rows(v_dim):
    tile_rows = 2 * SUBLANES
    return -(-(v_dim + 1) // tile_rows) * tile_rows


def _front_kernel(x_ref, cos_ref, sin_ref, prew_ref, win_ref, qn_ref, kvn_ref, wuq_ref, wuqr_ref,
                  wuk_ref, wuvt_ref, q_ref, k_ref, vt_ref, ckv_ref, kpe_ref, qkv_ref, z_ref, ba_ref,
                  *, ql, kvl, qkv_w, z_w, n_heads, v_dim, q_scale):
    xb = _rms(x_ref[...], prew_ref[...]).astype(BF16)

    def proj(c0, c1):
        return _dot(xb, win_ref[:, c0:c1])

    o = ql + kvl
    cq = proj(0, ql)
    ckv = _rms(proj(ql, o), kvn_ref[...])
    kr = proj(o, o + LANES)
    kr_rot = proj(o + LANES, o + 2 * LANES)
    ba_ref[...] = proj(o + 2 * LANES, o + 3 * LANES)
    o += 3 * LANES
    qkv_ref[...] = proj(o, o + qkv_w)
    z_ref[...] = proj(o + qkv_w, o + qkv_w + z_w)

    ckv_ref[...] = ckv
    cos = cos_ref[...]
    sin = sin_ref[...]
    kpe = kr * cos + kr_rot * sin
    kpe_ref[...] = kpe

    cqn = _rms(cq, qn_ref[...]).astype(BF16)
    ckvb = ckv.astype(BF16)
    q_lin = _dot(cqn, wuq_ref[...])
    q_rot = _dot(cqn, wuqr_ref[...])
    k_nope = _dot(ckvb, wuk_ref[...])
    vr = _vt_rows(v_dim)
    vt = _dot_nt(wuvt_ref[...], ckvb)
    ones_row = jnp.where(lax.broadcasted_iota(jnp.int32, (vr, 1), 0) == v_dim, 1.0, 0.0)
    for h in range(n_heads):
        sl = slice(h * LANES, (h + 1) * LANES)
        q_ref[h] = ((q_lin[:, sl] * cos + q_rot[:, sl] * sin) * q_scale).astype(BF16)
        k_ref[h] = (k_nope[:, sl] + kpe).astype(BF16)
        vt_ref[h] = (vt[h * vr:(h + 1) * vr, :] + ones_row).astype(BF16)


def _front(x, cos_t, sin_t, lw, l, dims):
    r, d = x.shape
    tm = ROW_TILE
    nh = dims["n_heads"]
    row = lambda w: pl.BlockSpec((tm, w), lambda i: (i, 0))
    wspec = lambda a: _resident((None,) + a.shape[1:], lambda i: (l,) + (0,) * (a.ndim - 1))
    out_shape = (
        jax.ShapeDtypeStruct((nh, r, LANES), BF16),
        jax.ShapeDtypeStruct((nh, r, LANES), BF16),
        jax.ShapeDtypeStruct((nh, _vt_rows(dims["v_dim"]), r), BF16),
        jax.ShapeDtypeStruct((r, dims["kvl"]), F32),
        jax.ShapeDtypeStruct((r, LANES), F32),
        jax.ShapeDtypeStruct((r, dims["qkv_w"]), F32),
        jax.ShapeDtypeStruct((r, dims["z_w"]), F32),
        jax.ShapeDtypeStruct((r, LANES), F32),
    )
    out_specs = (
        pl.BlockSpec((nh, tm, LANES), lambda i: (0, i, 0)),
        pl.BlockSpec((nh, tm, LANES), lambda i: (0, i, 0)),
        pl.BlockSpec((nh, _vt_rows(dims["v_dim"]), tm), lambda i: (0, 0, i)),
        row(dims["kvl"]), row(LANES), row(dims["qkv_w"]), row(dims["z_w"]), row(LANES),
    )
    weights = [lw["pre_mix_norm"], lw["w_in"], lw["q_norm"], lw["kv_norm"], lw["w_uq"], lw["w_uq_rot"],
               lw["w_uk"], lw["w_uvt"]]
    kern = functools.partial(_front_kernel, ql=dims["ql"], kvl=dims["kvl"], qkv_w=dims["qkv_w"],
                             z_w=dims["z_w"], n_heads=nh, v_dim=dims["v_dim"], q_scale=dims["q_scale"])
    return pl.pallas_call(
        kern, out_shape=out_shape, grid=(r // tm,),
        in_specs=[row(d), row(LANES), row(LANES)] + [wspec(w) for w in weights],
        out_specs=out_specs, compiler_params=_params(1), name=f"front_l{l}",
    )(x, cos_t, sin_t, *weights)


def _attn_kernel(q_ref, k_ref, vt_ref, o_ref, acc_ref, m_ref, s_ref, *, tile, fr0, n_meta, v_dim):
    sup = pl.program_id(1)
    is_frame = sup >= 1
    streams = [(hh, c) for hh in range(2) for c in range(2)]
    q = {(hh, c): q_ref[hh, c * tile:(c + 1) * tile, :] for hh, c in streams}

    def key_off(j):
        return pl.multiple_of(fr0 + j * tile, tile)

    def scores(hh, c, off):
        return _dot_nt(k_ref[hh, pl.ds(off, tile), :], q[hh, c])

    def update(hh, c, s, off, masked):
        if masked:
            kv_c = lax.broadcasted_iota(jnp.int32, s.shape, 0) // CHUNK
            q_c = lax.broadcasted_iota(jnp.int32, s.shape, 1) // CHUNK
            s = jnp.where(kv_c <= q_c, s, -jnp.inf)
        m_prev = m_ref[hh, c]
        m_new = jnp.maximum(m_prev, jnp.max(s, axis=0, keepdims=True))
        alpha = jnp.exp2(m_prev - m_new)
        p = jnp.exp2(s - m_new).astype(BF16)
        acc_ref[hh, c] = alpha * acc_ref[hh, c] + _dot(vt_ref[hh, :, pl.ds(off, tile)], p)
        m_ref[hh, c] = m_new

    mb = LANES
    s_meta = [_dot_nt(k_ref[hh, fr0 - mb:fr0, :], q[hh, c]) for hh, c in streams]
    for n, (hh, c) in enumerate(streams):
        s_ref[0, n] = scores(hh, c, key_off(0))
    for (hh, c), s in zip(streams, s_meta):
        rid = lax.broadcasted_iota(jnp.int32, s.shape, 0)
        s = jnp.where(rid >= mb - n_meta, s, -jnp.inf)
        m0 = jnp.max(s, axis=0, keepdims=True)
        m_ref[hh, c] = m0
        acc_ref[hh, c] = _dot(vt_ref[hh, :, fr0 - mb:fr0], jnp.exp2(s - m0).astype(BF16))

    def half(j_cur, slot_cur, slot_next):
        off_c, off_n = key_off(j_cur), key_off(j_cur + 1)
        for n, (hh, c) in enumerate(streams):
            s_ref[slot_next, n] = scores(hh, c, off_n)
            update(hh, c, s_ref[slot_cur, n], off_c, False)

    def body(t, carry):
        half(2 * t, 0, 1)
        half(2 * t + 1, 1, 0)
        return carry

    first = 2 * (sup - 1)
    lax.fori_loop(0, jnp.where(is_frame, sup - 1, 0), body, 0)

    @pl.when(is_frame)
    def _():
        off_a, off_b = key_off(first), key_off(first + 1)
        s_b = [scores(hh, 1, off_b) for hh in range(2)]
        for hh in range(2):
            update(hh, 0, s_ref[0, 2 * hh], off_a, True)
            update(hh, 1, s_ref[0, 2 * hh + 1], off_a, False)
        for hh in range(2):
            update(hh, 1, s_b[hh], off_b, True)

    for c in range(2):
        o = jnp.concatenate([acc_ref[hh, c, 0:v_dim, :] / acc_ref[hh, c, v_dim:v_dim + 1, :]
                             for hh in range(2)], axis=0)
        o_ref[c * tile:(c + 1) * tile, :] = o.T.astype(o_ref.dtype)


def _attention(q, k, vt, dims):
    nh, r, _ = q.shape
    tile = ROW_TILE
    v_dim = dims["v_dim"]
    vr = _vt_rows(v_dim)
    kern = functools.partial(_attn_kernel, tile=tile, fr0=dims["fr0"], n_meta=dims["n_meta"], v_dim=v_dim)
    return pl.pallas_call(
        kern, out_shape=jax.ShapeDtypeStruct((r, nh * v_dim), BF16),
        grid=(nh // 2, r // (2 * tile)),
        in_specs=[
            pl.BlockSpec((2, 2 * tile, LANES), lambda p, i: (p, i, 0)),
            _resident((2, r, LANES), lambda p, i: (p, 0, 0)),
            _resident((2, vr, r), lambda p, i: (p, 0, 0)),
        ],
        out_specs=pl.BlockSpec((2 * tile, 2 * v_dim), lambda p, i: (i, p)),
        scratch_shapes=[pltpu.VMEM((2, 2, vr, tile), F32), pltpu.VMEM((2, 2, 1, tile), F32),
                        pltpu.VMEM((2, 4, tile, tile), F32)],
        compiler_params=_params(2), name="prompt_attention",
    )(q, k, vt)


SATTN_BATCH = 2


def _sattn_kernel(q_ref, ckv_ref, kpe_ref, clat_ref, cpet_ref, wukt_ref, wuv_ref, o_in_ref, o_ref,
                  *, n_heads, nope, rope):
    del o_in_ref
    t = DEC_CHUNK
    staged = []
    for b in range(clat_ref.shape[0]):
        rows = slice(b * t, (b + 1) * t)
        clat = clat_ref[b].astype(BF16)
        cpet = cpet_ref[b].astype(BF16)
        ckv = ckv_ref[rows, :].astype(BF16)
        kpe = kpe_ref[rows, :].astype(BF16)
        qs = [q_ref[h, rows, :] for h in range(n_heads)]
        q_all = jnp.concatenate(qs, axis=0)
        q_lat = jnp.concatenate([_dot(qs[h][:, :nope], wukt_ref[h]) for h in range(n_heads)],
                                axis=0).astype(BF16)
        q_pe = q_all.astype(F32)[:, nope:nope + rope].astype(BF16)
        s_past = _dot_nt(q_lat, clat) + _dot(q_pe, cpet)
        s_new = _dot_nt(q_lat, ckv) + _dot_nt(q_all, kpe)
        staged.append((rows, clat, ckv, s_past, s_new))
    for rows, clat, ckv, s_past, s_new in staged:
        m = jnp.maximum(jnp.max(s_past, axis=-1, keepdims=True), jnp.max(s_new, axis=-1, keepdims=True))
        p_past = jnp.exp2(s_past - m)
        p_new = jnp.exp2(s_new - m)
        den = jnp.sum(p_past, axis=-1, keepdims=True) + jnp.sum(p_new, axis=-1, keepdims=True)
        o_lat = ((_dot(p_past.astype(BF16), clat) + _dot(p_new.astype(BF16), ckv)) / den).astype(BF16)
        o = jnp.concatenate([_dot(o_lat[h * t:(h + 1) * t], wuv_ref[h]) for h in range(n_heads)], axis=1)
        o_ref[rows, :] = o.astype(o_ref.dtype)


def _sample_attention(q, ckv, kpe, cache_lat, cache_pet, lw, l, o_mla, dims):
    nh = dims["n_heads"]
    sb = SATTN_BATCH
    t = sb * DEC_CHUNK
    nb = cache_lat.shape[1]
    past, kvl = cache_lat.shape[2:]
    rope = cache_pet.shape[2]
    kern = functools.partial(_sattn_kernel, n_heads=nh, nope=dims["nope"], rope=rope)
    return pl.pallas_call(
        kern, out_shape=jax.ShapeDtypeStruct(o_mla.shape, o_mla.dtype), grid=(nb // sb,),
        in_specs=[
            pl.BlockSpec((nh, t, LANES), lambda b: (0, b, 0)),
            pl.BlockSpec((t, kvl), lambda b: (b, 0)),
            pl.BlockSpec((t, LANES), lambda b: (b, 0)),
            pl.BlockSpec((None, sb, past, kvl), lambda b: (l, b, 0, 0)),
            pl.BlockSpec((None, sb, rope, past), lambda b: (l, b, 0, 0)),
            _resident((None,) + lw["w_uk_t"].shape[1:], lambda b: (l, 0, 0, 0)),
            _resident((None,) + lw["w_uv_h"].shape[1:], lambda b: (l, 0, 0, 0)),
            pl.BlockSpec(memory_space=pl.ANY),
        ],
        out_specs=pl.BlockSpec((t, o_mla.shape[1]), lambda b: (b, 0)),
        input_output_aliases={7: 0},
        compiler_params=_params(1), name=f"sample_attention_l{l}",
    )(q, ckv, kpe, cache_lat, cache_pet, lw["w_uk_t"], lw["w_uv_h"], o_mla)


def _split3(x):
    a = x.astype(BF16)
    r1 = x - a.astype(F32)
    b = r1.astype(BF16)
    return a, b, (r1 - b.astype(F32)).astype(BF16)


def _stack_heads(u, base, n_heads):
    return jnp.concatenate([u[:, base + h * LANES: base + (h + 1) * LANES] for h in range(n_heads)], axis=0)


def _gdn_prep(u, ba, alog, dtb, nh, c, consts):
    rows = u.shape[0]
    nblk = rows // c
    dk = LANES
    tri, incl, strict, eye = consts
    q_st = _stack_heads(u, 0, nh)
    k_st = _stack_heads(u, nh * dk, nh)
    v_st = _stack_heads(u, 2 * nh * dk, nh)
    qn = q_st * lax.rsqrt(jnp.sum(q_st * q_st, axis=-1, keepdims=True) + EPS) * (dk ** -0.5)
    kn = k_st * lax.rsqrt(jnp.sum(k_st * k_st, axis=-1, keepdims=True) + EPS)

    lane = lax.broadcasted_iota(jnp.int32, ba.shape, 1)
    beta = jax.nn.sigmoid(ba)
    x = ba + dtb
    softplus = jnp.maximum(x, 0.0) + jnp.log1p(jnp.exp(-jnp.abs(x)))
    g = jnp.where((lane >= nh) & (lane < 2 * nh), -jnp.exp(alog) * softplus, 0.0)
    g1, g2, g3 = _split3(g)
    gc = _dot(tri, g1) + _dot(tri, g2) + _dot(tri, g3)

    def col(a, lane0):
        return jnp.concatenate([jnp.broadcast_to(a[:, lane0 + h: lane0 + h + 1], (rows, LANES))
                                for h in range(nh)], axis=0)

    b_st = col(beta, 0)
    g_st = col(gc, nh)
    gl_st = jnp.concatenate(
        [jnp.broadcast_to(g_st[h * rows + (s + 1) * c - 1: h * rows + (s + 1) * c], (c, LANES))
         for h in range(nh) for s in range(nblk)], axis=0)
    eg = jnp.exp(g_st)
    n = nh * rows
    g_col = jnp.concatenate([g_st] * (n // LANES), axis=1)
    g_row = jnp.broadcast_to(g_st.T[0:1, :], (n, n))
    decay = jnp.exp(jnp.where(incl, g_col - g_row, -jnp.inf))
    knb = kn.astype(BF16)
    kk = _dot_nt(knb, knb)
    qk = _dot_nt(qn.astype(BF16), knb)
    b_col = jnp.concatenate([b_st] * (n // LANES), axis=1)
    a0 = jnp.where(strict, -(b_col * kk * decay), 0.0)
    return dict(
        a=a0, t=eye + a0,
        vk=jnp.concatenate([v_st * b_st, kn * b_st * eg], axis=1).astype(BF16),
        aqk=(qk * decay).astype(BF16), qg=(qn * eg).astype(BF16),
        kd=(kn * jnp.exp(gl_st - g_st)).astype(BF16), g_st=g_st)


def _gdn_chunks(us, bas, zs, alog, dtb, gnw, states, c):
    rows = us[0].shape[0]
    nh = len(states[0])
    nblk = rows // c
    n = nh * rows
    rr = lax.broadcasted_iota(jnp.int32, (rows, rows), 0)
    cc = lax.broadcasted_iota(jnp.int32, (rows, rows), 1)
    tri = jnp.where(((rr // c) == (cc // c)) & (rr >= cc), 1.0, 0.0).astype(BF16)
    r2 = lax.broadcasted_iota(jnp.int32, (n, n), 0)
    c2 = lax.broadcasted_iota(jnp.int32, (n, n), 1)
    same = (r2 // c) == (c2 // c)
    consts = (tri, same & (r2 >= c2), same & (r2 > c2), jnp.where(r2 == c2, 1.0, 0.0))

    blocks = [(h, s) for h in range(nh) for s in range(nblk)]

    def solve(pre, uw):
        a_bf = [p["a"].astype(BF16) for p in pre]
        t_mat = [p["t"] for p in pre]
        for _ in range(int(math.log2(c)) - 1):
            a_bf = [_dot(ab, ab).astype(BF16) for ab in a_bf]
            yield
            t_mat = [t + _dot(t.astype(BF16), ab) for t, ab in zip(t_mat, a_bf)]
            yield
        uw.extend(_dot(t.astype(BF16), p["vk"]) for t, p in zip(t_mat, pre))
        yield

    box = [states]
    outs = []

    def scan(pre, uw, zs_g):
        for p, uw_c, z in zip(pre, uw, zs_g):
            st_in = box[0]
            u_st, w_st = uw_c[:, :LANES], uw_c[:, LANES:]
            ws_qs = [_dot(jnp.concatenate([w_st[h * rows + s * c: h * rows + (s + 1) * c].astype(BF16),
                                           p["qg"][h * rows + s * c: h * rows + (s + 1) * c]], axis=0),
                          st_in[s][h].astype(BF16)) for h, s in blocks]
            yield
            v_new = [u_st[h * rows + s * c: h * rows + (s + 1) * c] - x[:c] for (h, s), x in zip(blocks, ws_qs)]
            st_out = [[None] * nh for _ in range(nblk)]
            for (h, s), vn in zip(blocks, v_new):
                r0 = h * rows + s * c
                g_last = jnp.exp(p["g_st"][r0 + c - 1: r0 + c])
                st_out[s][h] = st_in[s][h] * g_last + _dot_tn(p["kd"][r0:r0 + c], vn.astype(BF16))
            box[0] = st_out
            yield
            o_st = (jnp.concatenate([x[c:] for x in ws_qs], axis=0)
                    + _dot(p["aqk"], jnp.concatenate(v_new, axis=0).astype(BF16)))
            o_n = _rms(o_st, gnw)
            o = jnp.concatenate([o_n[h * rows:(h + 1) * rows] for h in range(nh)], axis=1)
            outs.append(o * _silu(z))
            yield

    def drive(*gens):
        gens = list(gens)
        while gens:
            for g in list(gens):
                try:
                    next(g)
                except StopIteration:
                    gens.remove(g)

    groups = [list(range(i, min(i + GDN_GROUP, len(us)))) for i in range(0, len(us), GDN_GROUP)]
    prev = None
    for g in groups:
        pre = [_gdn_prep(us[i], bas[i], alog, dtb, nh, c, consts) for i in g]
        uw = []
        drive(*([solve(pre, uw)] + ([scan(*prev)] if prev else [])))
        prev = (pre, uw, [zs[i] for i in g])
    drive(scan(*prev))
    return outs, box[0]


def _conv_silu(windows, cw):
    y = windows[0] * cw[0:1]
    for i in range(1, len(windows)):
        y = y + windows[i] * cw[i:i + 1]
    return _silu(y)


def _gdn_prompt_kernel(qkv_ref, ba_ref, z_ref, cw_ref, alog_ref, dtb_ref, gnw_ref, o_ref, s_out_ref,
                       xe_ref, s_ref, *, n_heads, conv_w, lead_rows):
    step = pl.program_id(0)
    rows = qkv_ref.shape[0]

    @pl.when(step == 0)
    def _():
        xe_ref[0:SUBLANES, :] = jnp.zeros((SUBLANES, xe_ref.shape[1]), F32)
        s_ref[...] = jnp.zeros(s_ref.shape, F32)

    seq_row = step * rows + lax.broadcasted_iota(jnp.int32, (rows, 1), 0)
    xe_ref[SUBLANES:SUBLANES + rows, :] = jnp.where(seq_row >= lead_rows, qkv_ref[...], 0.0)
    off = SUBLANES - (conv_w - 1)
    u = _conv_silu([xe_ref[off + i: off + i + rows, :] for i in range(conv_w)], cw_ref[...])
    xe_ref[0:SUBLANES, :] = xe_ref[rows:rows + SUBLANES, :]

    chunks = [slice(ck * CHUNK, (ck + 1) * CHUNK) for ck in range(rows // CHUNK)]
    outs, states = _gdn_chunks([u[sl] for sl in chunks], [ba_ref[sl, :] for sl in chunks],
                               [z_ref[sl, :] for sl in chunks], alog_ref[...], dtb_ref[...], gnw_ref[...],
                               [[s_ref[h] for h in range(n_heads)]], CHUNK)
    o_ref[...] = jnp.concatenate(outs, axis=0).astype(o_ref.dtype)
    for h in range(n_heads):
        s_ref[h] = states[0][h]

    @pl.when(step == pl.num_programs(0) - 1)
    def _():
        s_out_ref[...] = s_ref[...]


def _gdn_prompt(qkv, ba, z, lw, l, dims):
    nh = dims["gdn_heads"]
    rows = GDN_ROWS
    r = qkv.shape[0]
    base = dims["sm"] // rows
    st_shape = (nh, LANES, LANES)
    wspec = lambda a: _resident((None,) + a.shape[1:], lambda i: (l,) + (0,) * (a.ndim - 1))
    row = lambda w: pl.BlockSpec((rows, w), lambda i: (base + i, 0))
    kern = functools.partial(_gdn_prompt_kernel, n_heads=nh, conv_w=dims["conv_w"],
                             lead_rows=ROW_TILE - dims["n_meta"])
    params = [lw["conv_w"], lw["a_log"], lw["dt_bias"], lw["gdn_norm"]]
    return pl.pallas_call(
        kern,
        out_shape=(jax.ShapeDtypeStruct((r, z.shape[1]), BF16), jax.ShapeDtypeStruct(st_shape, F32)),
        grid=((r - dims["sm"]) // rows,),
        in_specs=[row(qkv.shape[1]), row(LANES), row(z.shape[1])] + [wspec(p) for p in params],
        out_specs=(row(z.shape[1]), pl.BlockSpec(st_shape, lambda i: (0, 0, 0))),
        scratch_shapes=[pltpu.VMEM((rows + SUBLANES, qkv.shape[1]), F32), pltpu.VMEM(st_shape, F32)],
        compiler_params=_params(1), name=f"gdn_prompt_l{l}",
    )(qkv, ba, z, *params)


def _gdn_sample_kernel(xe_ref, ba_ref, z_ref, cw_ref, alog_ref, dtb_ref, gnw_ref, s_in_ref, o_in_ref,
                       o_ref, s_out_ref, *, n_heads, conv_w):
    del o_in_ref
    nseq = xe_ref.shape[0]
    t = DEC_CHUNK
    cw = cw_ref[...]
    u = jnp.concatenate([_conv_silu([xe_ref[s, i:i + t, :] for i in range(conv_w)], cw) for s in range(nseq)],
                        axis=0)
    states = [[s_in_ref[s, h] for h in range(n_heads)] for s in range(nseq)]
    outs, states = _gdn_chunks([u], [ba_ref[...]], [z_ref[...]], alog_ref[...], dtb_ref[...], gnw_ref[...],
                               states, t)
    o_ref[...] = outs[0].astype(o_ref.dtype)
    for s in range(nseq):
        for h in range(n_heads):
            s_out_ref[s, h] = states[s][h]


def _gdn_sample(xe, ba, z, lw, l, s_in, o_gdn, dims):
    nh = dims["gdn_heads"]
    nseq = xe.shape[0]
    sb = CHUNK // DEC_CHUNK
    wspec = lambda a: _resident((None,) + a.shape[1:], lambda i: (l,) + (0,) * (a.ndim - 1))
    row = lambda w: pl.BlockSpec((CHUNK, w), lambda i: (i, 0))
    kern = functools.partial(_gdn_sample_kernel, n_heads=nh, conv_w=dims["conv_w"])
    params = [lw["conv_w"], lw["a_log"], lw["dt_bias"], lw["gdn_norm"]]
    st_shape = s_in.shape[1:]
    return pl.pallas_call(
        kern,
        out_shape=(jax.ShapeDtypeStruct(o_gdn.shape, o_gdn.dtype), jax.ShapeDtypeStruct(st_shape, F32)),
        grid=(nseq // sb,),
        in_specs=[pl.BlockSpec((sb,) + xe.shape[1:], lambda i: (i, 0, 0)), row(LANES), row(z.shape[1])]
                 + [wspec(p) for p in params]
                 + [pl.BlockSpec((None, sb) + st_shape[1:], lambda i: (l, i, 0, 0, 0)),
                    pl.BlockSpec(memory_space=pl.ANY)],
        out_specs=(row(o_gdn.shape[1]), pl.BlockSpec((sb,) + st_shape[1:], lambda i: (i, 0, 0, 0))),
        input_output_aliases={8: 0},
        compiler_params=_params(1), name=f"gdn_sample_l{l}",
    )(xe, ba, z, *params, s_in, o_gdn)


def _back_kernel(x_ref, om_ref, og_ref, wo_ref, pmn_ref, pfn_ref, wg_ref, wu_ref, wd_ref, pon_ref, y_ref,
                 *, ff_chunks):
    half = om_ref.shape[1]
    mix = _dot(om_ref[...], wo_ref[0:half, :]) + _dot(og_ref[...], wo_ref[half:, :])
    h = x_ref[...] + _rms(mix, pmn_ref[...])
    hn = _rms(h, pfn_ref[...]).astype(BF16)
    f = None
    for c0, c1 in ff_chunks:
        act = (_silu(_dot(hn, wg_ref[:, c0:c1])) * _dot(hn, wu_ref[:, c0:c1])).astype(BF16)
        part = _dot(act, wd_ref[c0:c1, :])
        f = part if f is None else f + part
    y_ref[...] = h + _rms(f, pon_ref[...])


def _back(x, o_mla, o_gdn, lw, l):
    r, d = x.shape
    tm = ROW_TILE
    d_ff = lw["w_gate"].shape[-1]
    n_tiles = d_ff // 256
    cut = 256 * ((n_tiles + 1) // 2)
    ff_chunks = ((0, cut), (cut, d_ff)) if cut < d_ff else ((0, d_ff),)
    row = lambda w: pl.BlockSpec((tm, w), lambda i: (i, 0))
    wspec = lambda a: _resident((None,) + a.shape[1:], lambda i: (l,) + (0,) * (a.ndim - 1))
    weights = [lw["w_o"], lw["post_mix_norm"], lw["pre_ffn_norm"], lw["w_gate"], lw["w_up"], lw["w_down"],
               lw["post_ffn_norm"]]
    return pl.pallas_call(
        functools.partial(_back_kernel, ff_chunks=ff_chunks),
        out_shape=jax.ShapeDtypeStruct((r, d), F32), grid=(r // tm,),
        in_specs=[row(d), row(o_mla.shape[1]), row(o_gdn.shape[1])] + [wspec(w) for w in weights],
        out_specs=row(d), compiler_params=_params(1), name=f"back_l{l}",
    )(x, o_mla, o_gdn, *weights)


def _prepare_weights(pre_mix_norm, w_in, q_norm, kv_norm, w_uq, w_uk, w_uv, conv_w, a_log, dt_bias,
                     gdn_norm, w_o, post_mix_norm, pre_ffn_norm, w_gate, w_up, w_down, post_ffn_norm, dims):
    depth = w_in.shape[0]
    ql, kvl, rope, nope, nh = dims["ql"], dims["kvl"], dims["rope"], dims["nope"], dims["n_heads"]
    gh = dims["gdn_heads"]
    half = rope // 2
    pad = LANES - nope - rope
    zeros = lambda *s: jnp.zeros((depth,) + s, F32)
    d = w_in.shape[1]

    o_pe = ql + kvl
    o_qkv = o_pe + rope
    o_z = o_qkv + dims["qkv_w"]
    o_b = o_z + dims["z_w"]
    kr = w_in[..., o_pe:o_qkv]
    kr_rot = jnp.concatenate([-kr[..., half:], kr[..., :half]], axis=-1)
    seg = lambda a: jnp.concatenate([zeros(d, nope), a, zeros(d, pad)], axis=-1)
    ba_seg = jnp.concatenate([w_in[..., o_b:o_b + 2 * gh], zeros(d, LANES - 2 * gh)], axis=-1)
    w_in_r = jnp.concatenate([w_in[..., :o_pe], seg(kr), seg(kr_rot), ba_seg, w_in[..., o_qkv:o_b]], axis=-1)

    uq = w_uq.reshape(depth, ql, nh, nope + rope)
    uq_n, uq_p = uq[..., :nope], uq[..., nope:]
    hz = lambda w: zeros(ql, nh, w)
    w_uq_p = jnp.concatenate([uq_n, uq_p, hz(pad)], axis=-1).reshape(depth, ql, nh * LANES)
    w_uq_rot = jnp.concatenate([hz(nope), -uq_p[..., half:], uq_p[..., :half], hz(pad)],
                               axis=-1).reshape(depth, ql, nh * LANES)
    w_uk_p = jnp.concatenate([w_uk, zeros(kvl, nh, LANES - nope)], axis=-1).reshape(depth, kvl, nh * LANES)
    vh = w_uv.shape[-1]
    vr = _vt_rows(vh)
    w_uv_p = jnp.concatenate([w_uv, zeros(kvl, nh, vr - vh)], axis=-1).reshape(depth, kvl, nh * vr)

    lane_vec = lambda a: jnp.concatenate([zeros(gh), a, zeros(LANES - 2 * gh)], axis=-1)[:, None, :]
    vec = lambda a: a[:, None, :].astype(F32)
    bf = lambda a: a.astype(BF16)
    return {
        "pre_mix_norm": vec(pre_mix_norm), "w_in": bf(w_in_r), "q_norm": vec(q_norm), "kv_norm": vec(kv_norm),
        "w_uq": bf(w_uq_p), "w_uq_rot": bf(w_uq_rot), "w_uk": bf(w_uk_p),
        "w_uvt": bf(jnp.swapaxes(w_uv_p, 1, 2)),
        "w_uk_t": bf(jnp.transpose(w_uk, (0, 2, 3, 1))), "w_uv_h": bf(jnp.transpose(w_uv, (0, 2, 1, 3))),
        "conv_w": conv_w.astype(F32), "a_log": lane_vec(a_log), "dt_bias": lane_vec(dt_bias),
        "gdn_norm": vec(gdn_norm), "w_o": bf(w_o), "post_mix_norm": vec(post_mix_norm),
        "pre_ffn_norm": vec(pre_ffn_norm), "w_gate": bf(w_gate), "w_up": bf(w_up), "w_down": bf(w_down),
        "post_ffn_norm": vec(post_ffn_norm),
    }


def _rope_tables(pos, dims):
    nope, rope = dims["nope"], dims["rope"]
    half = rope // 2
    inv = ROPE_THETA ** (-jnp.arange(half, dtype=F32) / half)
    ang = pos[:, None] * inv[None, :]
    c, s = jnp.cos(ang), jnp.sin(ang)
    n = pos.shape[0]
    pad = jnp.zeros((n, LANES - nope - rope), F32)
    cos_t = jnp.concatenate([jnp.ones((n, nope), F32), c, c, pad], axis=-1)
    sin_t = jnp.concatenate([jnp.zeros((n, nope), F32), s, s, pad], axis=-1)
    return cos_t, sin_t


def kernel(x_prompt, x_sample, cache_mla_latent, cache_mla_krope, state_gdn, state_gdn_conv, meta_tokens,
           pre_mix_norm, w_in, q_norm, kv_norm, w_uq, w_uk, w_uv, conv_w, a_log, dt_bias, gdn_norm, w_o,
           post_mix_norm, pre_ffn_norm, w_gate, w_up, w_down, post_ffn_norm):
    batch, seq, d = x_prompt.shape
    nb, t, _ = x_sample.shape
    depth, _, past, kvl = cache_mla_latent.shape
    n_meta = meta_tokens.shape[0]
    nh, nope = w_uk.shape[2], w_uk.shape[3]
    rope = cache_mla_krope.shape[-1]
    gh, dk, dv = state_gdn.shape[2:]
    cw = conv_w.shape[1]
    sm = nb * t
    fr0 = sm + ROW_TILE
    mt0 = fr0 - n_meta
    r = fr0 + seq
    assert batch == 1 and t == DEC_CHUNK and seq % (2 * ROW_TILE) == 0 and fr0 % (2 * ROW_TILE) == 0
    assert n_meta <= CHUNK and nh % 2 == 0 and nope + rope <= LANES and 2 * w_uv.shape[3] == LANES
    assert dk == LANES and dv == LANES and gh * CHUNK == GDN_STACK and cw - 1 <= SUBLANES and 2 * gh <= LANES
    assert sm % GDN_ROWS == 0 and (r - sm) % GDN_ROWS == 0 and n_meta + cw - 1 <= ROW_TILE
    assert nb % SATTN_BATCH == 0
    dims = dict(n_meta=n_meta, n_heads=nh, nope=nope, rope=rope, v_dim=w_uv.shape[3], ql=q_norm.shape[1],
                kvl=kvl, gdn_heads=gh, conv_w=cw, qkv_w=gh * (2 * dk + dv), z_w=gh * dv,
                q_scale=(nope + rope) ** -0.5 * math.log2(math.e), sm=sm, fr0=fr0)

    lw = _prepare_weights(pre_mix_norm, w_in, q_norm, kv_norm, w_uq, w_uk, w_uv, conv_w, a_log, dt_bias,
                          gdn_norm, w_o, post_mix_norm, pre_ffn_norm, w_gate, w_up, w_down, post_ffn_norm, dims)

    x = jnp.concatenate([x_sample.reshape(sm, d), jnp.zeros((ROW_TILE - n_meta, d), F32),
                         meta_tokens.astype(F32), x_prompt[0]], axis=0)
    pos = jnp.concatenate([jnp.tile(n_meta + past + jnp.arange(t), nb), jnp.zeros((ROW_TILE - n_meta,), jnp.int32),
                           jnp.arange(n_meta + seq)]).astype(F32)
    cos_t, sin_t = _rope_tables(pos, dims)

    qkv_w = dims["qkv_w"]
    cache_pet = jnp.swapaxes(cache_mla_krope, 2, 3)
    outs = [[] for _ in range(8)]
    for l in range(depth):
        q, k, vt, ckv, kpe, qkv, z, ba = _front(x, cos_t, sin_t, lw, l, dims)

        o_mla = _attention(q, k, vt, dims)
        o_mla = _sample_attention(q, ckv, kpe, cache_mla_latent, cache_pet, lw, l, o_mla, dims)

        o_gdn, s_prompt = _gdn_prompt(qkv, ba, z, lw, l, dims)
        xe_s = jnp.concatenate([state_gdn_conv[l].astype(F32), qkv[:sm].reshape(nb, t, qkv_w)], axis=1)
        o_gdn, s_sample = _gdn_sample(xe_s, ba, z, lw, l, state_gdn, o_gdn, dims)

        x = _back(x, o_mla, o_gdn, lw, l)

        kpe_r = kpe[:, nope:nope + rope]
        outs[0].append(ckv[mt0:][None])
        outs[1].append(kpe_r[mt0:][None])
        outs[2].append(s_prompt[None])
        outs[3].append(qkv[r - (cw - 1):][None])
        outs[4].append(ckv[:sm].reshape(nb, t, kvl))
        outs[5].append(kpe_r[:sm].reshape(nb, t, rope))
        outs[6].append(s_sample)
        outs[7].append(xe_s[:, t:])
    y_prompt = x[fr0:][None]
    y_sample = x[:sm].reshape(nb, t, d)
    return (y_prompt, y_sample) + tuple(jnp.stack(o) for o in outs)
```

```python
import functools
import math

import jax
import jax.numpy as jnp
from jax import lax
from jax.experimental import pallas as pl
from jax.experimental.pallas import tpu as pltpu

F32 = jnp.float32
BF16 = jnp.bfloat16

EPS = 1e-6
ROPE_THETA = 10000.0
CHUNK = 64
DEC_CHUNK = 16

LANES = 128
SUBLANES = 8
ROW_TILE = 512
GDN_ROWS = 512
GDN_GROUP = 2
GDN_STACK = 256
VMEM_LIMIT = 56 * 1024 * 1024


def _dot(a, b):
    return jnp.dot(a, b, preferred_element_type=F32)


def _dot_nt(a, b):
    return lax.dot_general(a, b, (((1,), (1,)), ((), ())), preferred_element_type=F32)


def _dot_tn(a, b):
    return lax.dot_general(a, b, (((0,), (0,)), ((), ())), preferred_element_type=F32)


def _rms(x, w):
    return x * lax.rsqrt(jnp.mean(x * x, axis=-1, keepdims=True) + EPS) * w


def _silu(x):
    return x * jax.nn.sigmoid(x)


def _params(n_axes):
    return pltpu.CompilerParams(dimension_semantics=("arbitrary",) * n_axes,
                                vmem_limit_bytes=VMEM_LIMIT)


def _resident(shape, index_map):
    return pl.BlockSpec(shape, index_map, pipeline_mode=pl.Buffered(1))


def _vt_rows(v_dim):
    tile_rows = 2 * SUBLANES
    return -(-(v_dim + 1) // tile_rows) * tile_rows


def _front_kernel(x_ref, cos_ref, sina_ref, sinb_ref, prew_ref, win_ref, qn_ref, kvn_ref, wuq_ref,
                  wuk_ref, wuvt_ref, q_ref, k_ref, vt_ref, ckv_ref, kpe_ref, qkv_ref, z_ref, ba_ref,
                  *, ql, kvl, qkv_w, z_w, n_heads, v_dim, half_rope, q_scale):
    xb = _rms(x_ref[...], prew_ref[...]).astype(BF16)

    def proj(c0, c1):
        return _dot(xb, win_ref[:, c0:c1])

    o = ql + kvl
    cq = proj(0, ql)
    ckv = _rms(proj(ql, o), kvn_ref[...])
    kr = proj(o, o + LANES)
    ba_ref[...] = proj(o + LANES, o + 2 * LANES)
    o += 2 * LANES
    qkv_ref[...] = proj(o, o + qkv_w)
    z_ref[...] = proj(o + qkv_w, o + qkv_w + z_w)

    cos, sin_a, sin_b = cos_ref[...], sina_ref[...], sinb_ref[...]

    def rope(x):
        return (x * cos + pltpu.roll(x, LANES - half_rope, axis=1) * sin_a
                + pltpu.roll(x, half_rope, axis=1) * sin_b)

    ckv_ref[...] = ckv
    kpe = rope(kr)
    kpe_ref[...] = kpe

    cqn = _rms(cq, qn_ref[...]).astype(BF16)
    ckvb = ckv.astype(BF16)
    q_lin = _dot(cqn, wuq_ref[...])
    k_nope = _dot(ckvb, wuk_ref[...])
    vr = _vt_rows(v_dim)
    vt = _dot_nt(wuvt_ref[...], ckvb)
    ones_row = jnp.where(lax.broadcasted_iota(jnp.int32, (vr, 1), 0) == v_dim, 1.0, 0.0)
    for h in range(n_heads):
        sl = slice(h * LANES, (h + 1) * LANES)
        q_ref[h] = (rope(q_lin[:, sl]) * q_scale).astype(BF16)
        k_ref[h] = (k_nope[:, sl] + kpe).astype(BF16)
        vt_ref[h] = (vt[h * vr:(h + 1) * vr, :] + ones_row).astype(BF16)


def _front(x, tables, lw, l, dims):
    r, d = x.shape
    tm = ROW_TILE
    nh = dims["n_heads"]
    row = lambda w: pl.BlockSpec((tm, w), lambda i: (i, 0))
    wspec = lambda a: _resident((None,) + a.shape[1:], lambda i: (l,) + (0,) * (a.ndim - 1))
    out_shape = (
        jax.ShapeDtypeStruct((nh, r, LANES), BF16),
        jax.ShapeDtypeStruct((nh, r, LANES), BF16),
        jax.ShapeDtypeStruct((nh, _vt_rows(dims["v_dim"]), r), BF16),
        jax.ShapeDtypeStruct((r, dims["kvl"]), F32),
        jax.ShapeDtypeStruct((r, LANES), F32),
        jax.ShapeDtypeStruct((r, dims["qkv_w"]), F32),
        jax.ShapeDtypeStruct((r, dims["z_w"]), F32),
        jax.ShapeDtypeStruct((r, LANES), F32),
    )
    out_specs = (
        pl.BlockSpec((nh, tm, LANES), lambda i: (0, i, 0)),
        pl.BlockSpec((nh, tm, LANES), lambda i: (0, i, 0)),
        pl.BlockSpec((nh, _vt_rows(dims["v_dim"]), tm), lambda i: (0, 0, i)),
        row(dims["kvl"]), row(LANES), row(dims["qkv_w"]), row(dims["z_w"]), row(LANES),
    )
    weights = [lw["pre_mix_norm"], lw["w_in"], lw["q_norm"], lw["kv_norm"], lw["w_uq"], lw["w_uk"], lw["w_uvt"]]
    kern = functools.partial(_front_kernel, ql=dims["ql"], kvl=dims["kvl"], qkv_w=dims["qkv_w"],
                             z_w=dims["z_w"], n_heads=nh, v_dim=dims["v_dim"], half_rope=dims["rope"] // 2,
                             q_scale=dims["q_scale"])
    return pl.pallas_call(
        kern, out_shape=out_shape, grid=(r // tm,),
        in_specs=[row(d)] + [row(LANES)] * len(tables) + [wspec(w) for w in weights],
        out_specs=out_specs, compiler_params=_params(1), name=f"front_l{l}",
    )(x, *tables, *weights)


def _attn_kernel(q_ref, k_ref, vt_ref, o_ref, acc_ref, m_ref, s_ref, *, tile, fr0, n_meta, v_dim):
    sup = pl.program_id(1)
    is_frame = sup >= 1
    streams = [(hh, c) for hh in range(2) for c in range(2)]
    q = {(hh, c): q_ref[hh, c * tile:(c + 1) * tile, :] for hh, c in streams}

    def key_off(j):
        return pl.multiple_of(fr0 + j * tile, tile)

    def scores(hh, c, off):
        return _dot_nt(k_ref[hh, pl.ds(off, tile), :], q[hh, c])

    def update(hh, c, s, off, masked):
        if masked:
            kv_c = lax.broadcasted_iota(jnp.int32, s.shape, 0) // CHUNK
            q_c = lax.broadcasted_iota(jnp.int32, s.shape, 1) // CHUNK
            s = jnp.where(kv_c <= q_c, s, -jnp.inf)
        m_prev = m_ref[hh, c]
        m_new = jnp.maximum(m_prev, jnp.max(s, axis=0, keepdims=True))
        alpha = jnp.exp2(m_prev - m_new)
        p = jnp.exp2(s - m_new).astype(BF16)
        acc_ref[hh, c] = alpha * acc_ref[hh, c] + _dot(vt_ref[hh, :, pl.ds(off, tile)], p)
        m_ref[hh, c] = m_new

    mb = LANES
    s_meta = [_dot_nt(k_ref[hh, fr0 - mb:fr0, :], q[hh, c]) for hh, c in streams]
    for n, (hh, c) in enumerate(streams):
        s_ref[0, n] = scores(hh, c, key_off(0))
    for (hh, c), s in zip(streams, s_meta):
        rid = lax.broadcasted_iota(jnp.int32, s.shape, 0)
        s = jnp.where(rid >= mb - n_meta, s, -jnp.inf)
        m0 = jnp.max(s, axis=0, keepdims=True)
        m_ref[hh, c] = m0
        acc_ref[hh, c] = _dot(vt_ref[hh, :, fr0 - mb:fr0], jnp.exp2(s - m0).astype(BF16))

    def half(j_cur, slot_cur, slot_next):
        off_c, off_n = key_off(j_cur), key_off(j_cur + 1)
        for n, (hh, c) in enumerate(streams):
            s_ref[slot_next, n] = scores(hh, c, off_n)
            update(hh, c, s_ref[slot_cur, n], off_c, False)

    def body(t, carry):
        half(2 * t, 0, 1)
        half(2 * t + 1, 1, 0)
        return carry

    first = 2 * (sup - 1)
    lax.fori_loop(0, jnp.where(is_frame, sup - 1, 0), body, 0)

    @pl.when(is_frame)
    def _():
        off_a, off_b = key_off(first), key_off(first + 1)
        s_b = [scores(hh, 1, off_b) for hh in range(2)]
        for hh in range(2):
            update(hh, 0, s_ref[0, 2 * hh], off_a, True)
            update(hh, 1, s_ref[0, 2 * hh + 1], off_a, False)
        for hh in range(2):
            update(hh, 1, s_b[hh], off_b, True)

    for c in range(2):
        o = jnp.concatenate([acc_ref[hh, c, 0:v_dim, :] / acc_ref[hh, c, v_dim:v_dim + 1, :]
                             for hh in range(2)], axis=0)
        o_ref[c * tile:(c + 1) * tile, :] = o.T.astype(o_ref.dtype)


def _attention(q, k, vt, dims):
    nh, r, _ = q.shape
    tile = ROW_TILE
    v_dim = dims["v_dim"]
    vr = _vt_rows(v_dim)
    kern = functools.partial(_attn_kernel, tile=tile, fr0=dims["fr0"], n_meta=dims["n_meta"], v_dim=v_dim)
    return pl.pallas_call(
        kern, out_shape=jax.ShapeDtypeStruct((r, nh * v_dim), BF16),
        grid=(nh // 2, r // (2 * tile)),
        in_specs=[
            pl.BlockSpec((2, 2 * tile, LANES), lambda p, i: (p, i, 0)),
            _resident((2, r, LANES), lambda p, i: (p, 0, 0)),
            _resident((2, vr, r), lambda p, i: (p, 0, 0)),
        ],
        out_specs=pl.BlockSpec((2 * tile, 2 * v_dim), lambda p, i: (i, p)),
        scratch_shapes=[pltpu.VMEM((2, 2, vr, tile), F32), pltpu.VMEM((2, 2, 1, tile), F32),
                        pltpu.VMEM((2, 4, tile, tile), F32)],
        compiler_params=_params(2), name="prompt_attention",
    )(q, k, vt)


SATTN_BATCH = 2


def _sattn_kernel(q_ref, ckv_ref, kpe_ref, clat_ref, cpet_ref, wukt_ref, wuv_ref, o_in_ref, o_ref,
                  *, n_heads, nope, rope):
    del o_in_ref
    t = DEC_CHUNK
    staged = []
    for b in range(clat_ref.shape[0]):
        rows = slice(b * t, (b + 1) * t)
        clat = clat_ref[b].astype(BF16)
        cpet = cpet_ref[b].astype(BF16)
        ckv = ckv_ref[rows, :].astype(BF16)
        kpe = kpe_ref[rows, :].astype(BF16)
        qs = [q_ref[h, rows, :] for h in range(n_heads)]
        q_all = jnp.concatenate(qs, axis=0)
        q_lat = jnp.concatenate([_dot(qs[h][:, :nope], wukt_ref[h]) for h in range(n_heads)],
                                axis=0).astype(BF16)
        q_pe = q_all.astype(F32)[:, nope:nope + rope].astype(BF16)
        s_past = _dot_nt(q_lat, clat) + _dot(q_pe, cpet)
        s_new = _dot_nt(q_lat, ckv) + _dot_nt(q_all, kpe)
        staged.append((rows, clat, ckv, s_past, s_new))
    for rows, clat, ckv, s_past, s_new in staged:
        m = jnp.maximum(jnp.max(s_past, axis=-1, keepdims=True), jnp.max(s_new, axis=-1, keepdims=True))
        p_past = jnp.exp2(s_past - m)
        p_new = jnp.exp2(s_new - m)
        den = jnp.sum(p_past, axis=-1, keepdims=True) + jnp.sum(p_new, axis=-1, keepdims=True)
        o_lat = ((_dot(p_past.astype(BF16), clat) + _dot(p_new.astype(BF16), ckv)) / den).astype(BF16)
        o = jnp.concatenate([_dot(o_lat[h * t:(h + 1) * t], wuv_ref[h]) for h in range(n_heads)], axis=1)
        o_ref[rows, :] = o.astype(o_ref.dtype)


def _sample_attention(q, ckv, kpe, cache_lat, cache_pet, lw, l, o_mla, dims):
    nh = dims["n_heads"]
    sb = SATTN_BATCH
    t = sb * DEC_CHUNK
    nb = cache_lat.shape[1]
    past, kvl = cache_lat.shape[2:]
    rope = cache_pet.shape[2]
    kern = functools.partial(_sattn_kernel, n_heads=nh, nope=dims["nope"], rope=rope)
    return pl.pallas_call(
        kern, out_shape=jax.ShapeDtypeStruct(o_mla.shape, o_mla.dtype), grid=(nb // sb,),
        in_specs=[
            pl.BlockSpec((nh, t, LANES), lambda b: (0, b, 0)),
            pl.BlockSpec((t, kvl), lambda b: (b, 0)),
            pl.BlockSpec((t, LANES), lambda b: (b, 0)),
            pl.BlockSpec((None, sb, past, kvl), lambda b: (l, b, 0, 0)),
            pl.BlockSpec((None, sb, rope, past), lambda b: (l, b, 0, 0)),
            _resident((None,) + lw["w_uk_t"].shape[1:], lambda b: (l, 0, 0, 0)),
            _resident((None,) + lw["w_uv_h"].shape[1:], lambda b: (l, 0, 0, 0)),
            pl.BlockSpec(memory_space=pl.ANY),
        ],
        out_specs=pl.BlockSpec((t, o_mla.shape[1]), lambda b: (b, 0)),
        input_output_aliases={7: 0},
        compiler_params=_params(1), name=f"sample_attention_l{l}",
    )(q, ckv, kpe, cache_lat, cache_pet, lw["w_uk_t"], lw["w_uv_h"], o_mla)


def _split3(x):
    a = x.astype(BF16)
    r1 = x - a.astype(F32)
    b = r1.astype(BF16)
    return a, b, (r1 - b.astype(F32)).astype(BF16)


def _stack_heads(u, base, n_heads):
    return jnp.concatenate([u[:, base + h * LANES: base + (h + 1) * LANES] for h in range(n_heads)], axis=0)


def _gdn_prep(us, bas, alog, dtb, nh, c, consts, out):
    rows = us[0].shape[0]
    nblk = rows // c
    dk = LANES
    n = nh * rows
    tri, incl, strict, eye = consts
    idx = range(len(us))

    def unit(x):
        return x * lax.rsqrt(jnp.sum(x * x, axis=-1, keepdims=True) + EPS)

    qn = [unit(_stack_heads(u, 0, nh)) * (dk ** -0.5) for u in us]
    kn = [unit(_stack_heads(u, nh * dk, nh)) for u in us]
    yield
    lane = lax.broadcasted_iota(jnp.int32, bas[0].shape, 1)
    g_lane = (lane >= nh) & (lane < 2 * nh)
    beta, gc = [], []
    for ba in bas:
        beta.append(jax.nn.sigmoid(ba))
        x = ba + dtb
        softplus = jnp.maximum(x, 0.0) + jnp.log1p(jnp.exp(-jnp.abs(x)))
        g1, g2, g3 = _split3(jnp.where(g_lane, -jnp.exp(alog) * softplus, 0.0))
        gc.append(_dot(tri, g1) + _dot(tri, g2) + _dot(tri, g3))
    knb = [x.astype(BF16) for x in kn]
    kk = [_dot_nt(x, x) for x in knb]
    qk = [_dot_nt(qn[i].astype(BF16), knb[i]) for i in idx]
    yield

    def col(a, lane0):
        return jnp.concatenate([jnp.broadcast_to(a[:, lane0 + h: lane0 + h + 1], (rows, LANES))
                                for h in range(nh)], axis=0)

    b_st = [col(x, 0) for x in beta]
    g_st = [col(x, nh) for x in gc]
    gl_st = [jnp.concatenate(
        [jnp.broadcast_to(x[h * rows + (s + 1) * c - 1: h * rows + (s + 1) * c], (c, LANES))
         for h in range(nh) for s in range(nblk)], axis=0) for x in g_st]
    eg = [jnp.exp(x) for x in g_st]
    yield
    decay = [jnp.exp(jnp.where(incl, jnp.concatenate([x] * (n // LANES), axis=1)
                               - jnp.broadcast_to(x.T[0:1, :], (n, n)), -jnp.inf)) for x in g_st]
    yield
    a0 = [jnp.where(strict, -(jnp.concatenate([b_st[i]] * (n // LANES), axis=1) * kk[i] * decay[i]), 0.0)
          for i in idx]
    yield
    for i in idx:
        v_st = _stack_heads(us[i], 2 * nh * dk, nh)
        out.append(dict(
            a=a0[i], t=eye + a0[i],
            vk=jnp.concatenate([v_st * b_st[i], kn[i] * b_st[i] * eg[i]], axis=1).astype(BF16),
            aqk=(qk[i] * decay[i]).astype(BF16), qg=(qn[i] * eg[i]).astype(BF16),
            kd=(kn[i] * jnp.exp(gl_st[i] - g_st[i])).astype(BF16), g_st=g_st[i]))
    yield


def _gdn_chunks(us, bas, zs, alog, dtb, gnw, states, c):
    rows = us[0].shape[0]
    nh = len(states[0])
    nblk = rows // c
    n = nh * rows
    rr = lax.broadcasted_iota(jnp.int32, (rows, rows), 0)
    cc = lax.broadcasted_iota(jnp.int32, (rows, rows), 1)
    tri = jnp.where(((rr // c) == (cc // c)) & (rr >= cc), 1.0, 0.0).astype(BF16)
    r2 = lax.broadcasted_iota(jnp.int32, (n, n), 0)
    c2 = lax.broadcasted_iota(jnp.int32, (n, n), 1)
    same = (r2 // c) == (c2 // c)
    consts = (tri, same & (r2 >= c2), same & (r2 > c2), jnp.where(r2 == c2, 1.0, 0.0))

    blocks = [(h, s) for h in range(nh) for s in range(nblk)]

    def solve(pre, uw):
        a_bf = [p["a"].astype(BF16) for p in pre]
        t_mat = [p["t"] for p in pre]
        for _ in range(int(math.log2(c)) - 1):
            a_bf = [_dot(ab, ab).astype(BF16) for ab in a_bf]
            yield
            t_mat = [t + _dot(t.astype(BF16), ab) for t, ab in zip(t_mat, a_bf)]
            yield
        uw.extend(_dot(t.astype(BF16), p["vk"]) for t, p in zip(t_mat, pre))
        yield

    box = [states]
    outs = []

    def scan(pre, uw, zs_g):
        for p, uw_c, z in zip(pre, uw, zs_g):
            st_in = box[0]
            u_st, w_st = uw_c[:, :LANES], uw_c[:, LANES:]
            ws_qs = [_dot(jnp.concatenate([w_st[h * rows + s * c: h * rows + (s + 1) * c].astype(BF16),
                                           p["qg"][h * rows + s * c: h * rows + (s + 1) * c]], axis=0),
                          st_in[s][h].astype(BF16)) for h, s in blocks]
            yield
            v_new = [u_st[h * rows + s * c: h * rows + (s + 1) * c] - x[:c] for (h, s), x in zip(blocks, ws_qs)]
            st_out = [[None] * nh for _ in range(nblk)]
            for (h, s), vn in zip(blocks, v_new):
                r0 = h * rows + s * c
                g_last = jnp.exp(p["g_st"][r0 + c - 1: r0 + c])
                st_out[s][h] = st_in[s][h] * g_last + _dot_tn(p["kd"][r0:r0 + c], vn.astype(BF16))
            box[0] = st_out
            yield
            o_st = (jnp.concatenate([x[c:] for x in ws_qs], axis=0)
                    + _dot(p["aqk"], jnp.concatenate(v_new, axis=0).astype(BF16)))
            o_n = _rms(o_st, gnw)
            o = jnp.concatenate([o_n[h * rows:(h + 1) * rows] for h in range(nh)], axis=1)
            outs.append(o * _silu(z))
            yield

    def drive(*gens):
        gens = list(gens)
        while gens:
            for g in list(gens):
                try:
                    next(g)
                except StopIteration:
                    gens.remove(g)

    groups = [list(range(i, min(i + GDN_GROUP, len(us)))) for i in range(0, len(us), GDN_GROUP)]
    prep = lambda g, out: _gdn_prep([us[i] for i in g], [bas[i] for i in g], alog, dtb, nh, c, consts, out)
    prev = None
    pre_next = []
    drive(prep(groups[0], pre_next))
    for gi, g in enumerate(groups):
        pre, pre_next, uw = pre_next, [], []
        gens = [solve(pre, uw)]
        if prev:
            gens.append(scan(*prev))
        if gi + 1 < len(groups):
            gens.append(prep(groups[gi + 1], pre_next))
        drive(*gens)
        prev = (pre, uw, [zs[i] for i in g])
    drive(scan(*prev))
    return outs, box[0]


def _conv_silu(windows, cw):
    y = windows[0] * cw[0:1]
    for i in range(1, len(windows)):
        y = y + windows[i] * cw[i:i + 1]
    return _silu(y)


def _gdn_prompt_kernel(qkv_ref, ba_ref, z_ref, cw_ref, alog_ref, dtb_ref, gnw_ref, o_ref, s_out_ref,
                       xe_ref, s_ref, *, n_heads, conv_w, lead_rows):
    step = pl.program_id(0)
    rows = qkv_ref.shape[0]

    @pl.when(step == 0)
    def _():
        xe_ref[0:SUBLANES, :] = jnp.zeros((SUBLANES, xe_ref.shape[1]), F32)
        s_ref[...] = jnp.zeros(s_ref.shape, F32)

    lead_steps = lead_rows // rows

    @pl.when(step < lead_steps)
    def _():
        o_ref[...] = jnp.zeros(o_ref.shape, o_ref.dtype)

    @pl.when(step >= lead_steps)
    def _():
        row = step * rows + lax.broadcasted_iota(jnp.int32, (rows, 1), 0)
        xe_ref[SUBLANES:SUBLANES + rows, :] = jnp.where(row >= lead_rows, qkv_ref[...], 0.0)
        off = SUBLANES - (conv_w - 1)
        u = _conv_silu([xe_ref[off + i: off + i + rows, :] for i in range(conv_w)], cw_ref[...])
        xe_ref[0:SUBLANES, :] = xe_ref[rows:rows + SUBLANES, :]

        chunks = [slice(ck * CHUNK, (ck + 1) * CHUNK) for ck in range(rows // CHUNK)]
        outs, states = _gdn_chunks([u[sl] for sl in chunks], [ba_ref[sl, :] for sl in chunks],
                                   [z_ref[sl, :] for sl in chunks], alog_ref[...], dtb_ref[...], gnw_ref[...],
                                   [[s_ref[h] for h in range(n_heads)]], CHUNK)
        o_ref[...] = jnp.concatenate(outs, axis=0).astype(o_ref.dtype)
        for h in range(n_heads):
            s_ref[h] = states[0][h]

    @pl.when(step == pl.num_programs(0) - 1)
    def _():
        s_out_ref[...] = s_ref[...]


def _gdn_prompt(qkv, ba, z, lw, l, dims):
    nh = dims["gdn_heads"]
    rows = GDN_ROWS
    r = qkv.shape[0]
    st_shape = (nh, LANES, LANES)
    wspec = lambda a: _resident((None,) + a.shape[1:], lambda i: (l,) + (0,) * (a.ndim - 1))
    row = lambda w: pl.BlockSpec((rows, w), lambda i: (i, 0))
    kern = functools.partial(_gdn_prompt_kernel, n_heads=nh, conv_w=dims["conv_w"],
                             lead_rows=dims["fr0"] - dims["n_meta"])
    params = [lw["conv_w"], lw["a_log"], lw["dt_bias"], lw["gdn_norm"]]
    return pl.pallas_call(
        kern,
        out_shape=(jax.ShapeDtypeStruct((r, z.shape[1]), BF16), jax.ShapeDtypeStruct(st_shape, F32)),
        grid=(r // rows,),
        in_specs=[row(qkv.shape[1]), row(LANES), row(z.shape[1])] + [wspec(p) for p in params],
        out_specs=(row(z.shape[1]), pl.BlockSpec(st_shape, lambda i: (0, 0, 0))),
        scratch_shapes=[pltpu.VMEM((rows + SUBLANES, qkv.shape[1]), F32), pltpu.VMEM(st_shape, F32)],
        compiler_params=_params(1), name=f"gdn_prompt_l{l}",
    )(qkv, ba, z, *params)


def _gdn_sample_kernel(xe_ref, ba_ref, z_ref, cw_ref, alog_ref, dtb_ref, gnw_ref, s_in_ref, o_in_ref,
                       o_ref, s_out_ref, *, n_heads, conv_w):
    del o_in_ref
    nseq = xe_ref.shape[0]
    t = DEC_CHUNK
    cw = cw_ref[...]
    u = jnp.concatenate([_conv_silu([xe_ref[s, i:i + t, :] for i in range(conv_w)], cw) for s in range(nseq)],
                        axis=0)
    states = [[s_in_ref[s, h] for h in range(n_heads)] for s in range(nseq)]
    outs, states = _gdn_chunks([u], [ba_ref[...]], [z_ref[...]], alog_ref[...], dtb_ref[...], gnw_ref[...],
                               states, t)
    o_ref[...] = outs[0].astype(o_ref.dtype)
    for s in range(nseq):
        for h in range(n_heads):
            s_out_ref[s, h] = states[s][h]


def _gdn_sample(xe, ba, z, lw, l, s_in, o_gdn, dims):
    nh = dims["gdn_heads"]
    nseq = xe.shape[0]
    sb = CHUNK // DEC_CHUNK
    wspec = lambda a: _resident((None,) + a.shape[1:], lambda i: (l,) + (0,) * (a.ndim - 1))
    row = lambda w: pl.BlockSpec((CHUNK, w), lambda i: (i, 0))
    kern = functools.partial(_gdn_sample_kernel, n_heads=nh, conv_w=dims["conv_w"])
    params = [lw["conv_w"], lw["a_log"], lw["dt_bias"], lw["gdn_norm"]]
    st_shape = s_in.shape[1:]
    return pl.pallas_call(
        kern,
        out_shape=(jax.ShapeDtypeStruct(o_gdn.shape, o_gdn.dtype), jax.ShapeDtypeStruct(st_shape, F32)),
        grid=(nseq // sb,),
        in_specs=[pl.BlockSpec((sb,) + xe.shape[1:], lambda i: (i, 0, 0)), row(LANES), row(z.shape[1])]
                 + [wspec(p) for p in params]
                 + [pl.BlockSpec((None, sb) + st_shape[1:], lambda i: (l, i, 0, 0, 0)),
                    pl.BlockSpec(memory_space=pl.ANY)],
        out_specs=(row(o_gdn.shape[1]), pl.BlockSpec((sb,) + st_shape[1:], lambda i: (i, 0, 0, 0))),
        input_output_aliases={8: 0},
        compiler_params=_params(1), name=f"gdn_sample_l{l}",
    )(xe, ba, z, *params, s_in, o_gdn)


def _back_kernel(x_ref, om_ref, og_ref, wo_ref, pmn_ref, pfn_ref, wg_ref, wu_ref, wd_ref, pon_ref, y_ref,
                 *, ff_chunks):
    half = om_ref.shape[1]
    mix = _dot(om_ref[...], wo_ref[0:half, :]) + _dot(og_ref[...], wo_ref[half:, :])
    h = x_ref[...] + _rms(mix, pmn_ref[...])
    hn = _rms(h, pfn_ref[...]).astype(BF16)
    f = None
    for c0, c1 in ff_chunks:
        act = (_silu(_dot(hn, wg_ref[:, c0:c1])) * _dot(hn, wu_ref[:, c0:c1])).astype(BF16)
        part = _dot(act, wd_ref[c0:c1, :])
        f = part if f is None else f + part
    y_ref[...] = h + _rms(f, pon_ref[...])


def _back(x, o_mla, o_gdn, lw, l):
    r, d = x.shape
    tm = ROW_TILE
    d_ff = lw["w_gate"].shape[-1]
    n_tiles = d_ff // 256
    cut = 256 * ((n_tiles + 1) // 2)
    ff_chunks = ((0, cut), (cut, d_ff)) if cut < d_ff else ((0, d_ff),)
    row = lambda w: pl.BlockSpec((tm, w), lambda i: (i, 0))
    wspec = lambda a: _resident((None,) + a.shape[1:], lambda i: (l,) + (0,) * (a.ndim - 1))
    weights = [lw["w_o"], lw["post_mix_norm"], lw["pre_ffn_norm"], lw["w_gate"], lw["w_up"], lw["w_down"],
               lw["post_ffn_norm"]]
    return pl.pallas_call(
        functools.partial(_back_kernel, ff_chunks=ff_chunks),
        out_shape=jax.ShapeDtypeStruct((r, d), F32), grid=(r // tm,),
        in_specs=[row(d), row(o_mla.shape[1]), row(o_gdn.shape[1])] + [wspec(w) for w in weights],
        out_specs=row(d), compiler_params=_params(1), name=f"back_l{l}",
    )(x, o_mla, o_gdn, *weights)


def _prepare_weights(pre_mix_norm, w_in, q_norm, kv_norm, w_uq, w_uk, w_uv, conv_w, a_log, dt_bias,
                     gdn_norm, w_o, post_mix_norm, pre_ffn_norm, w_gate, w_up, w_down, post_ffn_norm, dims):
    depth = w_in.shape[0]
    ql, kvl, rope, nope, nh = dims["ql"], dims["kvl"], dims["rope"], dims["nope"], dims["n_heads"]
    gh = dims["gdn_heads"]
    pad = LANES - nope - rope
    zeros = lambda *s: jnp.zeros((depth,) + s, F32)
    d = w_in.shape[1]

    o_pe = ql + kvl
    o_qkv = o_pe + rope
    o_z = o_qkv + dims["qkv_w"]
    o_b = o_z + dims["z_w"]
    kr_seg = jnp.concatenate([zeros(d, nope), w_in[..., o_pe:o_qkv], zeros(d, pad)], axis=-1)
    ba_seg = jnp.concatenate([w_in[..., o_b:o_b + 2 * gh], zeros(d, LANES - 2 * gh)], axis=-1)
    w_in_r = jnp.concatenate([w_in[..., :o_pe], kr_seg, ba_seg, w_in[..., o_qkv:o_b]], axis=-1)

    uq = w_uq.reshape(depth, ql, nh, nope + rope)
    w_uq_p = jnp.concatenate([uq, zeros(ql, nh, pad)], axis=-1).reshape(depth, ql, nh * LANES)
    w_uk_p = jnp.concatenate([w_uk, zeros(kvl, nh, LANES - nope)], axis=-1).reshape(depth, kvl, nh * LANES)
    vh = w_uv.shape[-1]
    vr = _vt_rows(vh)
    w_uv_p = jnp.concatenate([w_uv, zeros(kvl, nh, vr - vh)], axis=-1).reshape(depth, kvl, nh * vr)

    lane_vec = lambda a: jnp.concatenate([zeros(gh), a, zeros(LANES - 2 * gh)], axis=-1)[:, None, :]
    vec = lambda a: a[:, None, :].astype(F32)
    bf = lambda a: a.astype(BF16)
    return {
        "pre_mix_norm": vec(pre_mix_norm), "w_in": bf(w_in_r), "q_norm": vec(q_norm), "kv_norm": vec(kv_norm),
        "w_uq": bf(w_uq_p), "w_uk": bf(w_uk_p),
        "w_uvt": bf(jnp.swapaxes(w_uv_p, 1, 2)),
        "w_uk_t": bf(jnp.transpose(w_uk, (0, 2, 3, 1))), "w_uv_h": bf(jnp.transpose(w_uv, (0, 2, 1, 3))),
        "conv_w": conv_w.astype(F32), "a_log": lane_vec(a_log), "dt_bias": lane_vec(dt_bias),
        "gdn_norm": vec(gdn_norm), "w_o": bf(w_o), "post_mix_norm": vec(post_mix_norm),
        "pre_ffn_norm": vec(pre_ffn_norm), "w_gate": bf(w_gate), "w_up": bf(w_up), "w_down": bf(w_down),
        "post_ffn_norm": vec(post_ffn_norm),
    }


def _rope_tables(pos, dims):
    nope, rope = dims["nope"], dims["rope"]
    half = rope // 2
    inv = ROPE_THETA ** (-jnp.arange(half, dtype=F32) / half)
    ang = pos[:, None] * inv[None, :]
    c, s = jnp.cos(ang), jnp.sin(ang)
    n = pos.shape[0]
    z = lambda w: jnp.zeros((n, w), F32)
    pad = LANES - nope - rope
    cos_t = jnp.concatenate([jnp.ones((n, nope), F32), c, c, z(pad)], axis=-1)
    sin_a = jnp.concatenate([z(nope), -s, z(half + pad)], axis=-1)
    sin_b = jnp.concatenate([z(nope + half), s, z(pad)], axis=-1)
    return cos_t, sin_a, sin_b


def kernel(x_prompt, x_sample, cache_mla_latent, cache_mla_krope, state_gdn, state_gdn_conv, meta_tokens,
           pre_mix_norm, w_in, q_norm, kv_norm, w_uq, w_uk, w_uv, conv_w, a_log, dt_bias, gdn_norm, w_o,
           post_mix_norm, pre_ffn_norm, w_gate, w_up, w_down, post_ffn_norm):
    batch, seq, d = x_prompt.shape
    nb, t, _ = x_sample.shape
    depth, _, past, kvl = cache_mla_latent.shape
    n_meta = meta_tokens.shape[0]
    nh, nope = w_uk.shape[2], w_uk.shape[3]
    rope = cache_mla_krope.shape[-1]
    gh, dk, dv = state_gdn.shape[2:]
    cw = conv_w.shape[1]
    sm = nb * t
    fr0 = sm + ROW_TILE
    mt0 = fr0 - n_meta
    r = fr0 + seq
    assert batch == 1 and t == DEC_CHUNK and seq % (2 * ROW_TILE) == 0 and fr0 % (2 * ROW_TILE) == 0
    assert n_meta <= CHUNK and nh % 2 == 0 and nope + rope <= LANES and 2 * w_uv.shape[3] == LANES
    assert dk == LANES and dv == LANES and gh * CHUNK == GDN_STACK and cw - 1 <= SUBLANES and 2 * gh <= LANES
    assert sm % CHUNK == 0 and r % GDN_ROWS == 0 and n_meta + cw - 1 <= ROW_TILE
    assert nb % SATTN_BATCH == 0
    dims = dict(n_meta=n_meta, n_heads=nh, nope=nope, rope=rope, v_dim=w_uv.shape[3], ql=q_norm.shape[1],
                kvl=kvl, gdn_heads=gh, conv_w=cw, qkv_w=gh * (2 * dk + dv), z_w=gh * dv,
                q_scale=(nope + rope) ** -0.5 * math.log2(math.e), sm=sm, fr0=fr0)

    lw = _prepare_weights(pre_mix_norm, w_in, q_norm, kv_norm, w_uq, w_uk, w_uv, conv_w, a_log, dt_bias,
                          gdn_norm, w_o, post_mix_norm, pre_ffn_norm, w_gate, w_up, w_down, post_ffn_norm, dims)

    x = jnp.concatenate([x_sample.reshape(sm, d), jnp.zeros((ROW_TILE - n_meta, d), F32),
                         meta_tokens.astype(F32), x_prompt[0]], axis=0)
    pos = jnp.concatenate([jnp.tile(n_meta + past + jnp.arange(t), nb), jnp.zeros((ROW_TILE - n_meta,), jnp.int32),
                           jnp.arange(n_meta + seq)]).astype(F32)
    tables = _rope_tables(pos, dims)

    qkv_w = dims["qkv_w"]
    cache_pet = jnp.swapaxes(cache_mla_krope, 2, 3)
    outs = [[] for _ in range(8)]
    for l in range(depth):
        q, k, vt, ckv, kpe, qkv, z, ba = _front(x, tables, lw, l, dims)

        o_mla = _attention(q, k, vt, dims)
        o_mla = _sample_attention(q, ckv, kpe, cache_mla_latent, cache_pet, lw, l, o_mla, dims)

        o_gdn, s_prompt = _gdn_prompt(qkv, ba, z, lw, l, dims)
        xe_s = jnp.concatenate([state_gdn_conv[l].astype(F32), qkv[:sm].reshape(nb, t, qkv_w)], axis=1)
        o_gdn, s_sample = _gdn_sample(xe_s, ba, z, lw, l, state_gdn, o_gdn, dims)

        x = _back(x, o_mla, o_gdn, lw, l)

        kpe_r = kpe[:, nope:nope + rope]
        outs[0].append(ckv[mt0:][None])
        outs[1].append(kpe_r[mt0:][None])
        outs[2].append(s_prompt[None])
        outs[3].append(qkv[r - (cw - 1):][None])
        outs[4].append(ckv[:sm].reshape(nb, t, kvl))
        outs[5].append(kpe_r[:sm].reshape(nb, t, rope))
        outs[6].append(s_sample)
        outs[7].append(xe_s[:, t:])
    y_prompt = x[fr0:][None]
    y_sample = x[:sm].reshape(nb, t, d)
    return (y_prompt, y_sample) + tuple(jnp.stack(o) for o in outs)
```

```python
import functools
import math

import jax
import jax.numpy as jnp
from jax import lax
from jax.experimental import pallas as pl
from jax.experimental.pallas import tpu as pltpu

F32 = jnp.float32
BF16 = jnp.bfloat16

EPS = 1e-6
ROPE_THETA = 10000.0
CHUNK = 64
DEC_CHUNK = 16

LANES = 128
SUBLANES = 8
ROW_TILE = 512
GDN_ROWS = 512
GDN_GROUP = 2
GDN_STACK = 256
VMEM_LIMIT = 56 * 1024 * 1024


def _dot(a, b):
    return jnp.dot(a, b, preferred_element_type=F32)


def _dot_nt(a, b):
    return lax.dot_general(a, b, (((1,), (1,)), ((), ())), preferred_element_type=F32)


def _dot_tn(a, b):
    return lax.dot_general(a, b, (((0,), (0,)), ((), ())), preferred_element_type=F32)


def _rms(x, w):
    return x * lax.rsqrt(jnp.mean(x * x, axis=-1, keepdims=True) + EPS) * w


def _silu(x):
    return x * jax.nn.sigmoid(x)


def _params(n_axes):
    return pltpu.CompilerParams(dimension_semantics=("arbitrary",) * n_axes,
                                vmem_limit_bytes=VMEM_LIMIT)


def _resident(shape, index_map):
    return pl.BlockSpec(shape, index_map, pipeline_mode=pl.Buffered(1))


def _vt_rows(v_dim):
    tile_rows = 2 * SUBLANES
    return -(-(v_dim + 1) // tile_rows) * tile_rows


def _front_kernel(x_ref, cos_ref, sina_ref, sinb_ref, prew_ref, win_ref, qn_ref, kvn_ref, wuq_ref,
                  wuk_ref, wuvt_ref, q_ref, k_ref, vt_ref, ckv_ref, kpe_ref, qkv_ref, z_ref, ba_ref,
                  *, ql, kvl, qkv_w, z_w, n_heads, v_dim, half_rope, q_scale):
    xb = _rms(x_ref[...], prew_ref[...]).astype(BF16)

    def proj(c0, c1):
        return _dot(xb, win_ref[:, c0:c1])

    o = ql + kvl
    cq = proj(0, ql)
    ckv = _rms(proj(ql, o), kvn_ref[...])
    kr = proj(o, o + LANES)
    ba_ref[...] = proj(o + LANES, o + 2 * LANES)
    o += 2 * LANES
    qkv_ref[...] = proj(o, o + qkv_w)
    z_ref[...] = proj(o + qkv_w, o + qkv_w + z_w)

    cos, sin_a, sin_b = cos_ref[...], sina_ref[...], sinb_ref[...]

    def rope(x):
        return (x * cos + pltpu.roll(x, LANES - half_rope, axis=1) * sin_a
                + pltpu.roll(x, half_rope, axis=1) * sin_b)

    ckv_ref[...] = ckv
    kpe = rope(kr)
    kpe_ref[...] = kpe

    cqn = _rms(cq, qn_ref[...]).astype(BF16)
    ckvb = ckv.astype(BF16)
    q_lin = _dot(cqn, wuq_ref[...])
    k_nope = _dot(ckvb, wuk_ref[...])
    vr = _vt_rows(v_dim)
    vt = _dot_nt(wuvt_ref[...], ckvb)
    ones_row = jnp.where(lax.broadcasted_iota(jnp.int32, (vr, 1), 0) == v_dim, 1.0, 0.0)
    for h in range(n_heads):
        sl = slice(h * LANES, (h + 1) * LANES)
        q_ref[h] = (rope(q_lin[:, sl]) * q_scale).astype(BF16)
        k_ref[h] = (k_nope[:, sl] + kpe).astype(BF16)
        vt_ref[h] = (vt[h * vr:(h + 1) * vr, :] + ones_row).astype(BF16)


def _front(x, tables, lw, l, dims):
    r, d = x.shape
    tm = ROW_TILE
    nh = dims["n_heads"]
    row = lambda w: pl.BlockSpec((tm, w), lambda i: (i, 0))
    wspec = lambda a: _resident((None,) + a.shape[1:], lambda i: (l,) + (0,) * (a.ndim - 1))
    out_shape = (
        jax.ShapeDtypeStruct((nh, r, LANES), BF16),
        jax.ShapeDtypeStruct((nh, r, LANES), BF16),
        jax.ShapeDtypeStruct((nh, _vt_rows(dims["v_dim"]), r), BF16),
        jax.ShapeDtypeStruct((r, dims["kvl"]), F32),
        jax.ShapeDtypeStruct((r, LANES), F32),
        jax.ShapeDtypeStruct((r, dims["qkv_w"]), F32),
        jax.ShapeDtypeStruct((r, dims["z_w"]), F32),
        jax.ShapeDtypeStruct((r, LANES), F32),
    )
    out_specs = (
        pl.BlockSpec((nh, tm, LANES), lambda i: (0, i, 0)),
        pl.BlockSpec((nh, tm, LANES), lambda i: (0, i, 0)),
        pl.BlockSpec((nh, _vt_rows(dims["v_dim"]), tm), lambda i: (0, 0, i)),
        row(dims["kvl"]), row(LANES), row(dims["qkv_w"]), row(dims["z_w"]), row(LANES),
    )
    weights = [lw["pre_mix_norm"], lw["w_in"], lw["q_norm"], lw["kv_norm"], lw["w_uq"], lw["w_uk"], lw["w_uvt"]]
    kern = functools.partial(_front_kernel, ql=dims["ql"], kvl=dims["kvl"], qkv_w=dims["qkv_w"],
                             z_w=dims["z_w"], n_heads=nh, v_dim=dims["v_dim"], half_rope=dims["rope"] // 2,
                             q_scale=dims["q_scale"])
    return pl.pallas_call(
        kern, out_shape=out_shape, grid=(r // tm,),
        in_specs=[row(d)] + [row(LANES)] * len(tables) + [wspec(w) for w in weights],
        out_specs=out_specs, compiler_params=_params(1), name=f"front_l{l}",
    )(x, *tables, *weights)


def _attn_kernel(q_ref, k_ref, vt_ref, o_ref, acc_ref, m_ref, s_ref, *, tile, fr0, n_meta, v_dim):
    sup = pl.program_id(1)
    is_frame = sup >= 1
    streams = [(hh, c) for hh in range(2) for c in range(2)]
    q = {(hh, c): q_ref[hh, c * tile:(c + 1) * tile, :] for hh, c in streams}

    def key_off(j):
        return pl.multiple_of(fr0 + j * tile, tile)

    def scores(hh, c, off):
        return _dot_nt(k_ref[hh, pl.ds(off, tile), :], q[hh, c])

    def update(hh, c, s, off, masked):
        if masked:
            kv_c = lax.broadcasted_iota(jnp.int32, s.shape, 0) // CHUNK
            q_c = lax.broadcasted_iota(jnp.int32, s.shape, 1) // CHUNK
            s = jnp.where(kv_c <= q_c, s, -jnp.inf)
        m_prev = m_ref[hh, c]
        m_new = jnp.maximum(m_prev, jnp.max(s, axis=0, keepdims=True))
        alpha = jnp.exp2(m_prev - m_new)
        p = jnp.exp2(s - m_new).astype(BF16)
        acc_ref[hh, c] = alpha * acc_ref[hh, c] + _dot(vt_ref[hh, :, pl.ds(off, tile)], p)
        m_ref[hh, c] = m_new

    mb = LANES
    s_meta = [_dot_nt(k_ref[hh, fr0 - mb:fr0, :], q[hh, c]) for hh, c in streams]
    for n, (hh, c) in enumerate(streams):
        s_ref[0, n] = scores(hh, c, key_off(0))
    for (hh, c), s in zip(streams, s_meta):
        rid = lax.broadcasted_iota(jnp.int32, s.shape, 0)
        s = jnp.where(rid >= mb - n_meta, s, -jnp.inf)
        m0 = jnp.max(s, axis=0, keepdims=True)
        m_ref[hh, c] = m0
        acc_ref[hh, c] = _dot(vt_ref[hh, :, fr0 - mb:fr0], jnp.exp2(s - m0).astype(BF16))

    def half(j_cur, slot_cur, slot_next):
        off_c, off_n = key_off(j_cur), key_off(j_cur + 1)
        for n, (hh, c) in enumerate(streams):
            s_ref[slot_next, n] = scores(hh, c, off_n)
            update(hh, c, s_ref[slot_cur, n], off_c, False)

    def body(t, carry):
        half(2 * t, 0, 1)
        half(2 * t + 1, 1, 0)
        return carry

    first = 2 * (sup - 1)
    lax.fori_loop(0, jnp.where(is_frame, sup - 1, 0), body, 0)

    @pl.when(is_frame)
    def _():
        off_a, off_b = key_off(first), key_off(first + 1)
        s_b = [scores(hh, 1, off_b) for hh in range(2)]
        for hh in range(2):
            update(hh, 0, s_ref[0, 2 * hh], off_a, True)
            update(hh, 1, s_ref[0, 2 * hh + 1], off_a, False)
        for hh in range(2):
            update(hh, 1, s_b[hh], off_b, True)

    for c in range(2):
        o = jnp.concatenate([acc_ref[hh, c, 0:v_dim, :] / acc_ref[hh, c, v_dim:v_dim + 1, :]
                             for hh in range(2)], axis=0)
        o_ref[c * tile:(c + 1) * tile, :] = o.T.astype(o_ref.dtype)


def _attention(q, k, vt, dims):
    nh, r, _ = q.shape
    tile = ROW_TILE
    v_dim = dims["v_dim"]
    vr = _vt_rows(v_dim)
    kern = functools.partial(_attn_kernel, tile=tile, fr0=dims["fr0"], n_meta=dims["n_meta"], v_dim=v_dim)
    return pl.pallas_call(
        kern, out_shape=jax.ShapeDtypeStruct((r, nh * v_dim), BF16),
        grid=(nh // 2, r // (2 * tile)),
        in_specs=[
            pl.BlockSpec((2, 2 * tile, LANES), lambda p, i: (p, i, 0)),
            _resident((2, r, LANES), lambda p, i: (p, 0, 0)),
            _resident((2, vr, r), lambda p, i: (p, 0, 0)),
        ],
        out_specs=pl.BlockSpec((2 * tile, 2 * v_dim), lambda p, i: (i, p)),
        scratch_shapes=[pltpu.VMEM((2, 2, vr, tile), F32), pltpu.VMEM((2, 2, 1, tile), F32),
                        pltpu.VMEM((2, 4, tile, tile), F32)],
        compiler_params=_params(2), name="prompt_attention",
    )(q, k, vt)


SATTN_BATCH = 2


def _sattn_kernel(q_ref, ckv_ref, kpe_ref, clat_ref, cpet_ref, wukt_ref, wuv_ref, o_in_ref, o_ref,
                  *, n_heads, nope, rope):
    del o_in_ref
    t = DEC_CHUNK
    staged = []
    for b in range(clat_ref.shape[0]):
        rows = slice(b * t, (b + 1) * t)
        clat = clat_ref[b].astype(BF16)
        cpet = cpet_ref[b].astype(BF16)
        ckv = ckv_ref[rows, :].astype(BF16)
        kpe = kpe_ref[rows, :].astype(BF16)
        qs = [q_ref[h, rows, :] for h in range(n_heads)]
        q_all = jnp.concatenate(qs, axis=0)
        q_lat = jnp.concatenate([_dot(qs[h][:, :nope], wukt_ref[h]) for h in range(n_heads)],
                                axis=0).astype(BF16)
        q_pe = q_all.astype(F32)[:, nope:nope + rope].astype(BF16)
        s_past = _dot_nt(q_lat, clat) + _dot(q_pe, cpet)
        s_new = _dot_nt(q_lat, ckv) + _dot_nt(q_all, kpe)
        staged.append((rows, clat, ckv, s_past, s_new))
    for rows, clat, ckv, s_past, s_new in staged:
        m = jnp.maximum(jnp.max(s_past, axis=-1, keepdims=True), jnp.max(s_new, axis=-1, keepdims=True))
        p_past = jnp.exp2(s_past - m)
        p_new = jnp.exp2(s_new - m)
        den = jnp.sum(p_past, axis=-1, keepdims=True) + jnp.sum(p_new, axis=-1, keepdims=True)
        o_lat = ((_dot(p_past.astype(BF16), clat) + _dot(p_new.astype(BF16), ckv)) / den).astype(BF16)
        o = jnp.concatenate([_dot(o_lat[h * t:(h + 1) * t], wuv_ref[h]) for h in range(n_heads)], axis=1)
        o_ref[rows, :] = o.astype(o_ref.dtype)


def _sample_attention(q, ckv, kpe, cache_lat, cache_pet, lw, l, o_mla, dims):
    nh = dims["n_heads"]
    sb = SATTN_BATCH
    t = sb * DEC_CHUNK
    nb = cache_lat.shape[1]
    past, kvl = cache_lat.shape[2:]
    rope = cache_pet.shape[2]
    kern = functools.partial(_sattn_kernel, n_heads=nh, nope=dims["nope"], rope=rope)
    return pl.pallas_call(
        kern, out_shape=jax.ShapeDtypeStruct(o_mla.shape, o_mla.dtype), grid=(nb // sb,),
        in_specs=[
            pl.BlockSpec((nh, t, LANES), lambda b: (0, b, 0)),
            pl.BlockSpec((t, kvl), lambda b: (b, 0)),
            pl.BlockSpec((t, LANES), lambda b: (b, 0)),
            pl.BlockSpec((None, sb, past, kvl), lambda b: (l, b, 0, 0)),
            pl.BlockSpec((None, sb, rope, past), lambda b: (l, b, 0, 0)),
            _resident((None,) + lw["w_uk_t"].shape[1:], lambda b: (l, 0, 0, 0)),
            _resident((None,) + lw["w_uv_h"].shape[1:], lambda b: (l, 0, 0, 0)),
            pl.BlockSpec(memory_space=pl.ANY),
        ],
        out_specs=pl.BlockSpec((t, o_mla.shape[1]), lambda b: (b, 0)),
        input_output_aliases={7: 0},
        compiler_params=_params(1), name=f"sample_attention_l{l}",
    )(q, ckv, kpe, cache_lat, cache_pet, lw["w_uk_t"], lw["w_uv_h"], o_mla)


def _split3(x):
    a = x.astype(BF16)
    r1 = x - a.astype(F32)
    b = r1.astype(BF16)
    return a, b, (r1 - b.astype(F32)).astype(BF16)


def _stack_heads(u, base, n_heads):
    return jnp.concatenate([u[:, base + h * LANES: base + (h + 1) * LANES] for h in range(n_heads)], axis=0)


def _gdn_prep(us, bas, alog, dtb, nh, c, consts, out):
    rows = bas[0].shape[0]
    nblk = rows // c
    dk = LANES
    n = nh * rows
    tri, incl, strict = consts
    idx = range(len(us))

    def unit(x):
        return x * lax.rsqrt(jnp.sum(x * x, axis=-1, keepdims=True) + EPS)

    qn = [unit(u[0]) * (dk ** -0.5) for u in us]
    kn = [unit(u[1]) for u in us]
    yield
    lane = lax.broadcasted_iota(jnp.int32, bas[0].shape, 1)
    g_lane = (lane >= nh) & (lane < 2 * nh)
    beta, gc = [], []
    for ba in bas:
        beta.append(jax.nn.sigmoid(ba))
        x = ba + dtb
        softplus = jnp.maximum(x, 0.0) + jnp.log1p(jnp.exp(-jnp.abs(x)))
        g1, g2, g3 = _split3(jnp.where(g_lane, -jnp.exp(alog) * softplus, 0.0))
        gc.append(_dot(tri, g1) + _dot(tri, g2) + _dot(tri, g3))
    knb = [x.astype(BF16) for x in kn]
    kk = [_dot_nt(x, x) for x in knb]
    qk = [_dot_nt(qn[i].astype(BF16), knb[i]) for i in idx]
    yield

    def col(a, lane0):
        return jnp.concatenate([jnp.broadcast_to(a[:, lane0 + h: lane0 + h + 1], (rows, LANES))
                                for h in range(nh)], axis=0)

    b_st = [col(x, 0) for x in beta]
    g_st = [col(x, nh) for x in gc]
    gl_st = [jnp.concatenate(
        [jnp.broadcast_to(x[h * rows + (s + 1) * c - 1: h * rows + (s + 1) * c], (c, LANES))
         for h in range(nh) for s in range(nblk)], axis=0) for x in g_st]
    eg = [jnp.exp(x) for x in g_st]
    yield
    decay = [jnp.exp(jnp.where(incl, jnp.concatenate([x] * (n // LANES), axis=1)
                               - jnp.broadcast_to(x.T[0:1, :], (n, n)), -jnp.inf)) for x in g_st]
    yield
    a0 = [jnp.where(strict, -(jnp.concatenate([b_st[i]] * (n // LANES), axis=1) * kk[i] * decay[i]), 0.0)
          for i in idx]
    yield
    for i in idx:
        v_st = us[i][2]
        out.append(dict(
            a=a0[i],
            vk=jnp.concatenate([v_st * b_st[i], kn[i] * b_st[i] * eg[i]], axis=1).astype(BF16),
            aqk=(qk[i] * decay[i]).astype(BF16), qg=(qn[i] * eg[i]).astype(BF16),
            kd=(kn[i] * jnp.exp(gl_st[i] - g_st[i])).astype(BF16), g_st=g_st[i]))
    yield


def _gdn_chunks(us, bas, zs, alog, dtb, gnw, states, c):
    rows = bas[0].shape[0]
    nh = len(states[0])
    nblk = rows // c
    n = nh * rows
    rr = lax.broadcasted_iota(jnp.int32, (rows, rows), 0)
    cc = lax.broadcasted_iota(jnp.int32, (rows, rows), 1)
    tri = jnp.where(((rr // c) == (cc // c)) & (rr >= cc), 1.0, 0.0).astype(BF16)
    r2 = lax.broadcasted_iota(jnp.int32, (n, n), 0)
    c2 = lax.broadcasted_iota(jnp.int32, (n, n), 1)
    same = (r2 // c) == (c2 // c)
    consts = (tri, same & (r2 >= c2), same & (r2 > c2))

    blocks = [(h, s) for h in range(nh) for s in range(nblk)]

    head_of_lane = lax.broadcasted_iota(jnp.int32, (rows, n), 1) // rows
    same_head = (r2 // rows) == (c2 // rows)
    eye_c = jnp.where(lax.broadcasted_iota(jnp.int32, (rows, n), 0)
                      == lax.broadcasted_iota(jnp.int32, (rows, n), 1) % rows, 1.0, 0.0)

    def compress(x):
        parts = [jnp.where(head_of_lane == h, x[h * rows:(h + 1) * rows, :], 0.0) for h in range(nh)]
        return functools.reduce(lambda a, b: a + b, parts)

    def expand(x_c):
        return jnp.where(same_head, jnp.concatenate([x_c] * nh, axis=0), jnp.zeros((), x_c.dtype))

    def solve(pre, uw):
        a_c = [compress(p["a"]) for p in pre]
        t_c = [eye_c + a for a in a_c]
        a_bf = [a.astype(BF16) for a in a_c]
        a_bd = [expand(a) for a in a_bf]
        for _ in range(int(math.log2(c)) - 1):
            a_bf = [_dot(a, bd).astype(BF16) for a, bd in zip(a_bf, a_bd)]
            yield
            a_bd = [expand(a) for a in a_bf]
            t_c = [t + _dot(t.astype(BF16), bd) for t, bd in zip(t_c, a_bd)]
            yield
        uw.extend(_dot(expand(t.astype(BF16)), p["vk"]) for t, p in zip(t_c, pre))
        yield

    box = [states]
    outs = []

    def scan(pre, uw, zs_g):
        for p, uw_c, z in zip(pre, uw, zs_g):
            st_in = box[0]
            u_st, w_st = uw_c[:, :LANES], uw_c[:, LANES:]
            ws_qs = [_dot(jnp.concatenate([w_st[h * rows + s * c: h * rows + (s + 1) * c].astype(BF16),
                                           p["qg"][h * rows + s * c: h * rows + (s + 1) * c]], axis=0),
                          st_in[s][h].astype(BF16)) for h, s in blocks]
            yield
            v_new = [u_st[h * rows + s * c: h * rows + (s + 1) * c] - x[:c] for (h, s), x in zip(blocks, ws_qs)]
            st_out = [[None] * nh for _ in range(nblk)]
            for (h, s), vn in zip(blocks, v_new):
                r0 = h * rows + s * c
                g_last = jnp.exp(p["g_st"][r0 + c - 1: r0 + c])
                st_out[s][h] = st_in[s][h] * g_last + _dot_tn(p["kd"][r0:r0 + c], vn.astype(BF16))
            box[0] = st_out
            yield
            o_st = (jnp.concatenate([x[c:] for x in ws_qs], axis=0)
                    + _dot(p["aqk"], jnp.concatenate(v_new, axis=0).astype(BF16)))
            o_n = _rms(o_st, gnw)
            o = jnp.concatenate([o_n[h * rows:(h + 1) * rows] for h in range(nh)], axis=1)
            outs.append(o * _silu(z))
            yield

    def drive(*gens):
        gens = list(gens)
        while gens:
            for g in list(gens):
                try:
                    next(g)
                except StopIteration:
                    gens.remove(g)

    groups = [list(range(i, min(i + GDN_GROUP, len(us)))) for i in range(0, len(us), GDN_GROUP)]
    prep = lambda g, out: _gdn_prep([us[i] for i in g], [bas[i] for i in g], alog, dtb, nh, c, consts, out)
    prev = None
    pre_next = []
    drive(prep(groups[0], pre_next))
    for gi, g in enumerate(groups):
        pre, pre_next, uw = pre_next, [], []
        gens = [solve(pre, uw)]
        if prev:
            gens.append(scan(*prev))
        if gi + 1 < len(groups):
            gens.append(prep(groups[gi + 1], pre_next))
        drive(*gens)
        prev = (pre, uw, [zs[i] for i in g])
    drive(scan(*prev))
    return outs, box[0]


def _conv_silu(windows, cw):
    y = windows[0] * cw[0:1]
    for i in range(1, len(windows)):
        y = y + windows[i] * cw[i:i + 1]
    return _silu(y)


def _conv_silu_strided(qkv_ref, keep, cw, xe_ref, u_ref, conv_w):
    nslab, ext, _ = xe_ref.shape
    rows = ext - SUBLANES
    pitch = ext // SUBLANES
    for c in range(nslab):
        lanes = slice(c * LANES, (c + 1) * LANES)
        xe_ref[c, SUBLANES:ext, :] = jnp.where(keep, qkv_ref[:, lanes], 0.0)
        x = [xe_ref[c, pl.ds(j, SUBLANES, stride=pitch), :] for j in range(pitch)]
        before = [pltpu.roll(x[pitch - k], 1, axis=0) for k in range(1, conv_w)]
        w = [cw[i:i + 1, lanes] for i in range(conv_w)]
        for j in range(pitch):
            taps = [x[j - d] if j >= d else before[d - j - 1] for d in range(conv_w)]
            y = taps[conv_w - 1] * w[0]
            for i in range(1, conv_w):
                y = y + taps[conv_w - 1 - i] * w[i]
            u_ref[c, pl.ds(j, SUBLANES, stride=pitch), :] = _silu(y)
        xe_ref[c, 0:SUBLANES, :] = xe_ref[c, rows:ext, :]


def _gdn_prompt_kernel(qkv_ref, ba_ref, z_ref, cw_ref, alog_ref, dtb_ref, gnw_ref, o_ref, s_out_ref,
                       xe_ref, u_ref, s_ref, *, n_heads, conv_w, lead_rows):
    step = pl.program_id(0)
    rows = qkv_ref.shape[0]

    @pl.when(step == 0)
    def _():
        xe_ref[:, 0:SUBLANES, :] = jnp.zeros((xe_ref.shape[0], SUBLANES, LANES), F32)
        s_ref[...] = jnp.zeros(s_ref.shape, F32)

    lead_steps = lead_rows // rows

    @pl.when(step < lead_steps)
    def _():
        o_ref[...] = jnp.zeros(o_ref.shape, o_ref.dtype)

    @pl.when(step >= lead_steps)
    def _():
        row = step * rows + lax.broadcasted_iota(jnp.int32, (rows, 1), 0)
        _conv_silu_strided(qkv_ref, row >= lead_rows, cw_ref[...], xe_ref, u_ref, conv_w)

        chunks = [slice(ck * CHUNK, (ck + 1) * CHUNK) for ck in range(rows // CHUNK)]

        def stacked(sl, part):
            return jnp.concatenate([u_ref[part * n_heads + h, SUBLANES + sl.start:SUBLANES + sl.stop, :]
                                    for h in range(n_heads)], axis=0)

        outs, states = _gdn_chunks([tuple(stacked(sl, part) for part in range(3)) for sl in chunks],
                                   [ba_ref[sl, :] for sl in chunks],
                                   [z_ref[sl, :] for sl in chunks], alog_ref[...], dtb_ref[...], gnw_ref[...],
                                   [[s_ref[h] for h in range(n_heads)]], CHUNK)
        o_ref[...] = jnp.concatenate(outs, axis=0).astype(o_ref.dtype)
        for h in range(n_heads):
            s_ref[h] = states[0][h]

    @pl.when(step == pl.num_programs(0) - 1)
    def _():
        s_out_ref[...] = s_ref[...]


def _gdn_prompt(qkv, ba, z, lw, l, dims):
    nh = dims["gdn_heads"]
    rows = GDN_ROWS
    r = qkv.shape[0]
    st_shape = (nh, LANES, LANES)
    wspec = lambda a: _resident((None,) + a.shape[1:], lambda i: (l,) + (0,) * (a.ndim - 1))
    row = lambda w: pl.BlockSpec((rows, w), lambda i: (i, 0))
    kern = functools.partial(_gdn_prompt_kernel, n_heads=nh, conv_w=dims["conv_w"],
                             lead_rows=dims["fr0"] - dims["n_meta"])
    params = [lw["conv_w"], lw["a_log"], lw["dt_bias"], lw["gdn_norm"]]
    return pl.pallas_call(
        kern,
        out_shape=(jax.ShapeDtypeStruct((r, z.shape[1]), BF16), jax.ShapeDtypeStruct(st_shape, F32)),
        grid=(r // rows,),
        in_specs=[row(qkv.shape[1]), row(LANES), row(z.shape[1])] + [wspec(p) for p in params],
        out_specs=(row(z.shape[1]), pl.BlockSpec(st_shape, lambda i: (0, 0, 0))),
        scratch_shapes=[pltpu.VMEM((qkv.shape[1] // LANES, rows + SUBLANES, LANES), F32),
                        pltpu.VMEM((qkv.shape[1] // LANES, rows + SUBLANES, LANES), F32),
                        pltpu.VMEM(st_shape, F32)],
        compiler_params=_params(1), name=f"gdn_prompt_l{l}",
    )(qkv, ba, z, *params)


def _gdn_sample_kernel(xe_ref, ba_ref, z_ref, cw_ref, alog_ref, dtb_ref, gnw_ref, s_in_ref, o_in_ref,
                       o_ref, s_out_ref, *, n_heads, conv_w):
    del o_in_ref
    nseq = xe_ref.shape[0]
    t = DEC_CHUNK
    cw = cw_ref[...]
    u = jnp.concatenate([_conv_silu([xe_ref[s, i:i + t, :] for i in range(conv_w)], cw) for s in range(nseq)],
                        axis=0)
    states = [[s_in_ref[s, h] for h in range(n_heads)] for s in range(nseq)]
    dk = LANES
    qkv_st = tuple(_stack_heads(u, i * n_heads * dk, n_heads) for i in range(3))
    outs, states = _gdn_chunks([qkv_st], [ba_ref[...]], [z_ref[...]], alog_ref[...], dtb_ref[...], gnw_ref[...],
                               states, t)
    o_ref[...] = outs[0].astype(o_ref.dtype)
    for s in range(nseq):
        for h in range(n_heads):
            s_out_ref[s, h] = states[s][h]


def _gdn_sample(xe, ba, z, lw, l, s_in, o_gdn, dims):
    nh = dims["gdn_heads"]
    nseq = xe.shape[0]
    sb = CHUNK // DEC_CHUNK
    wspec = lambda a: _resident((None,) + a.shape[1:], lambda i: (l,) + (0,) * (a.ndim - 1))
    row = lambda w: pl.BlockSpec((CHUNK, w), lambda i: (i, 0))
    kern = functools.partial(_gdn_sample_kernel, n_heads=nh, conv_w=dims["conv_w"])
    params = [lw["conv_w"], lw["a_log"], lw["dt_bias"], lw["gdn_norm"]]
    st_shape = s_in.shape[1:]
    return pl.pallas_call(
        kern,
        out_shape=(jax.ShapeDtypeStruct(o_gdn.shape, o_gdn.dtype), jax.ShapeDtypeStruct(st_shape, F32)),
        grid=(nseq // sb,),
        in_specs=[pl.BlockSpec((sb,) + xe.shape[1:], lambda i: (i, 0, 0)), row(LANES), row(z.shape[1])]
                 + [wspec(p) for p in params]
                 + [pl.BlockSpec((None, sb) + st_shape[1:], lambda i: (l, i, 0, 0, 0)),
                    pl.BlockSpec(memory_space=pl.ANY)],
        out_specs=(row(o_gdn.shape[1]), pl.BlockSpec((sb,) + st_shape[1:], lambda i: (i, 0, 0, 0))),
        input_output_aliases={8: 0},
        compiler_params=_params(1), name=f"gdn_sample_l{l}",
    )(xe, ba, z, *params, s_in, o_gdn)


def _back_kernel(x_ref, om_ref, og_ref, wo_ref, pmn_ref, pfn_ref, wg_ref, wu_ref, wd_ref, pon_ref, y_ref,
                 *, ff_chunks):
    half = om_ref.shape[1]
    mix = _dot(om_ref[...], wo_ref[0:half, :]) + _dot(og_ref[...], wo_ref[half:, :])
    h = x_ref[...] + _rms(mix, pmn_ref[...])
    hn = _rms(h, pfn_ref[...]).astype(BF16)
    f = None
    for c0, c1 in ff_chunks:
        act = (_silu(_dot(hn, wg_ref[:, c0:c1])) * _dot(hn, wu_ref[:, c0:c1])).astype(BF16)
        part = _dot(act, wd_ref[c0:c1, :])
        f = part if f is None else f + part
    y_ref[...] = h + _rms(f, pon_ref[...])


def _back(x, o_mla, o_gdn, lw, l):
    r, d = x.shape
    tm = ROW_TILE
    d_ff = lw["w_gate"].shape[-1]
    n_tiles = d_ff // 256
    cut = 256 * ((n_tiles + 1) // 2)
    ff_chunks = ((0, cut), (cut, d_ff)) if cut < d_ff else ((0, d_ff),)
    row = lambda w: pl.BlockSpec((tm, w), lambda i: (i, 0))
    wspec = lambda a: _resident((None,) + a.shape[1:], lambda i: (l,) + (0,) * (a.ndim - 1))
    weights = [lw["w_o"], lw["post_mix_norm"], lw["pre_ffn_norm"], lw["w_gate"], lw["w_up"], lw["w_down"],
               lw["post_ffn_norm"]]
    return pl.pallas_call(
        functools.partial(_back_kernel, ff_chunks=ff_chunks),
        out_shape=jax.ShapeDtypeStruct((r, d), F32), grid=(r // tm,),
        in_specs=[row(d), row(o_mla.shape[1]), row(o_gdn.shape[1])] + [wspec(w) for w in weights],
        out_specs=row(d), compiler_params=_params(1), name=f"back_l{l}",
    )(x, o_mla, o_gdn, *weights)


def _prepare_weights(pre_mix_norm, w_in, q_norm, kv_norm, w_uq, w_uk, w_uv, conv_w, a_log, dt_bias,
                     gdn_norm, w_o, post_mix_norm, pre_ffn_norm, w_gate, w_up, w_down, post_ffn_norm, dims):
    depth = w_in.shape[0]
    ql, kvl, rope, nope, nh = dims["ql"], dims["kvl"], dims["rope"], dims["nope"], dims["n_heads"]
    gh = dims["gdn_heads"]
    pad = LANES - nope - rope
    zeros = lambda *s: jnp.zeros((depth,) + s, F32)
    d = w_in.shape[1]

    o_pe = ql + kvl
    o_qkv = o_pe + rope
    o_z = o_qkv + dims["qkv_w"]
    o_b = o_z + dims["z_w"]
    kr_seg = jnp.concatenate([zeros(d, nope), w_in[..., o_pe:o_qkv], zeros(d, pad)], axis=-1)
    ba_seg = jnp.concatenate([w_in[..., o_b:o_b + 2 * gh], zeros(d, LANES - 2 * gh)], axis=-1)
    w_in_r = jnp.concatenate([w_in[..., :o_pe], kr_seg, ba_seg, w_in[..., o_qkv:o_b]], axis=-1)

    uq = w_uq.reshape(depth, ql, nh, nope + rope)
    w_uq_p = jnp.concatenate([uq, zeros(ql, nh, pad)], axis=-1).reshape(depth, ql, nh * LANES)
    w_uk_p = jnp.concatenate([w_uk, zeros(kvl, nh, LANES - nope)], axis=-1).reshape(depth, kvl, nh * LANES)
    vh = w_uv.shape[-1]
    vr = _vt_rows(vh)
    w_uv_p = jnp.concatenate([w_uv, zeros(kvl, nh, vr - vh)], axis=-1).reshape(depth, kvl, nh * vr)

    lane_vec = lambda a: jnp.concatenate([zeros(gh), a, zeros(LANES - 2 * gh)], axis=-1)[:, None, :]
    vec = lambda a: a[:, None, :].astype(F32)
    bf = lambda a: a.astype(BF16)
    return {
        "pre_mix_norm": vec(pre_mix_norm), "w_in": bf(w_in_r), "q_norm": vec(q_norm), "kv_norm": vec(kv_norm),
        "w_uq": bf(w_uq_p), "w_uk": bf(w_uk_p),
        "w_uvt": bf(jnp.swapaxes(w_uv_p, 1, 2)),
        "w_uk_t": bf(jnp.transpose(w_uk, (0, 2, 3, 1))), "w_uv_h": bf(jnp.transpose(w_uv, (0, 2, 1, 3))),
        "conv_w": conv_w.astype(F32), "a_log": lane_vec(a_log), "dt_bias": lane_vec(dt_bias),
        "gdn_norm": vec(gdn_norm), "w_o": bf(w_o), "post_mix_norm": vec(post_mix_norm),
        "pre_ffn_norm": vec(pre_ffn_norm), "w_gate": bf(w_gate), "w_up": bf(w_up), "w_down": bf(w_down),
        "post_ffn_norm": vec(post_ffn_norm),
    }


def _rope_tables(pos, dims):
    nope, rope = dims["nope"], dims["rope"]
    half = rope // 2
    inv = ROPE_THETA ** (-jnp.arange(half, dtype=F32) / half)
    ang = pos[:, None] * inv[None, :]
    c, s = jnp.cos(ang), jnp.sin(ang)
    n = pos.shape[0]
    z = lambda w: jnp.zeros((n, w), F32)
    pad = LANES - nope - rope
    cos_t = jnp.concatenate([jnp.ones((n, nope), F32), c, c, z(pad)], axis=-1)
    sin_a = jnp.concatenate([z(nope), -s, z(half + pad)], axis=-1)
    sin_b = jnp.concatenate([z(nope + half), s, z(pad)], axis=-1)
    return cos_t, sin_a, sin_b


def kernel(x_prompt, x_sample, cache_mla_latent, cache_mla_krope, state_gdn, state_gdn_conv, meta_tokens,
           pre_mix_norm, w_in, q_norm, kv_norm, w_uq, w_uk, w_uv, conv_w, a_log, dt_bias, gdn_norm, w_o,
           post_mix_norm, pre_ffn_norm, w_gate, w_up, w_down, post_ffn_norm):
    batch, seq, d = x_prompt.shape
    nb, t, _ = x_sample.shape
    depth, _, past, kvl = cache_mla_latent.shape
    n_meta = meta_tokens.shape[0]
    nh, nope = w_uk.shape[2], w_uk.shape[3]
    rope = cache_mla_krope.shape[-1]
    gh, dk, dv = state_gdn.shape[2:]
    cw = conv_w.shape[1]
    sm = nb * t
    fr0 = sm + ROW_TILE
    mt0 = fr0 - n_meta
    r = fr0 + seq
    assert batch == 1 and t == DEC_CHUNK and seq % (2 * ROW_TILE) == 0 and fr0 % (2 * ROW_TILE) == 0
    assert n_meta <= CHUNK and nh % 2 == 0 and nope + rope <= LANES and 2 * w_uv.shape[3] == LANES
    assert dk == LANES and dv == LANES and gh * CHUNK == GDN_STACK and cw - 1 <= SUBLANES and 2 * gh <= LANES
    assert sm % CHUNK == 0 and r % GDN_ROWS == 0 and n_meta + cw - 1 <= ROW_TILE
    assert nb % SATTN_BATCH == 0
    dims = dict(n_meta=n_meta, n_heads=nh, nope=nope, rope=rope, v_dim=w_uv.shape[3], ql=q_norm.shape[1],
                kvl=kvl, gdn_heads=gh, conv_w=cw, qkv_w=gh * (2 * dk + dv), z_w=gh * dv,
                q_scale=(nope + rope) ** -0.5 * math.log2(math.e), sm=sm, fr0=fr0)

    lw = _prepare_weights(pre_mix_norm, w_in, q_norm, kv_norm, w_uq, w_uk, w_uv, conv_w, a_log, dt_bias,
                          gdn_norm, w_o, post_mix_norm, pre_ffn_norm, w_gate, w_up, w_down, post_ffn_norm, dims)

    x = jnp.concatenate([x_sample.reshape(sm, d), jnp.zeros((ROW_TILE - n_meta, d), F32),
                         meta_tokens.astype(F32), x_prompt[0]], axis=0)
    pos = jnp.concatenate([jnp.tile(n_meta + past + jnp.arange(t), nb), jnp.zeros((ROW_TILE - n_meta,), jnp.int32),
                           jnp.arange(n_meta + seq)]).astype(F32)
    tables = _rope_tables(pos, dims)

    qkv_w = dims["qkv_w"]
    cache_pet = jnp.swapaxes(cache_mla_krope, 2, 3)
    outs = [[] for _ in range(8)]
    for l in range(depth):
        q, k, vt, ckv, kpe, qkv, z, ba = _front(x, tables, lw, l, dims)

        o_mla = _attention(q, k, vt, dims)
        o_mla = _sample_attention(q, ckv, kpe, cache_mla_latent, cache_pet, lw, l, o_mla, dims)

        o_gdn, s_prompt = _gdn_prompt(qkv, ba, z, lw, l, dims)
        xe_s = jnp.concatenate([state_gdn_conv[l].astype(F32), qkv[:sm].reshape(nb, t, qkv_w)], axis=1)
        o_gdn, s_sample = _gdn_sample(xe_s, ba, z, lw, l, state_gdn, o_gdn, dims)

        x = _back(x, o_mla, o_gdn, lw, l)

        kpe_r = kpe[:, nope:nope + rope]
        outs[0].append(ckv[mt0:][None])
        outs[1].append(kpe_r[mt0:][None])
        outs[2].append(s_prompt[None])
        outs[3].append(qkv[r - (cw - 1):][None])
        outs[4].append(ckv[:sm].reshape(nb, t, kvl))
        outs[5].append(kpe_r[:sm].reshape(nb, t, rope))
        outs[6].append(s_sample)
        outs[7].append(xe_s[:, t:])
    y_prompt = x[fr0:][None]
    y_sample = x[:sm].reshape(nb, t, d)
    return (y_prompt, y_sample) + tuple(jnp.stack(o) for o in outs)
```

```python
import functools
import math

import jax
import jax.numpy as jnp
from jax import lax
from jax.experimental import pallas as pl
from jax.experimental.pallas import tpu as pltpu

F32 = jnp.float32
BF16 = jnp.bfloat16

EPS = 1e-6
ROPE_THETA = 10000.0
CHUNK = 64
DEC_CHUNK = 16

LANES = 128
SUBLANES = 8
ROW_TILE = 512
GDN_ROWS = 512
GDN_GROUP = 4
GDN_STACK = 256
VMEM_LIMIT = 56 * 1024 * 1024


def _dot(a, b):
    return jnp.dot(a, b, preferred_element_type=F32)


def _dot_nt(a, b):
    return lax.dot_general(a, b, (((1,), (1,)), ((), ())), preferred_element_type=F32)


def _dot_tn(a, b):
    return lax.dot_general(a, b, (((0,), (0,)), ((), ())), preferred_element_type=F32)


def _rms(x, w):
    return x * lax.rsqrt(jnp.mean(x * x, axis=-1, keepdims=True) + EPS) * w


def _silu(x):
    return x * jax.nn.sigmoid(x)


def _params(n_axes):
    return pltpu.CompilerParams(dimension_semantics=("arbitrary",) * n_axes,
                                vmem_limit_bytes=VMEM_LIMIT)


def _resident(shape, index_map):
    return pl.BlockSpec(shape, index_map, pipeline_mode=pl.Buffered(1))


def _vt_rows(v_dim):
    tile_rows = 2 * SUBLANES
    return -(-(v_dim + 1) // tile_rows) * tile_rows


def _front_kernel(x_ref, cos_ref, sina_ref, sinb_ref, prew_ref, win_ref, qn_ref, kvn_ref, wuq_ref,
                  wuk_ref, wuvt_ref, q_ref, k_ref, vt_ref, ckv_ref, kpe_ref, qkv_ref, z_ref, ba_ref,
                  *, ql, kvl, qkv_w, z_w, n_heads, v_dim, half_rope, q_scale):
    xb = _rms(x_ref[...], prew_ref[...]).astype(BF16)

    def proj(c0, c1):
        return _dot(xb, win_ref[:, c0:c1])

    o = ql + kvl
    cq = proj(0, ql)
    ckv = _rms(proj(ql, o), kvn_ref[...])
    kr = proj(o, o + LANES)
    ba_ref[...] = proj(o + LANES, o + 2 * LANES)
    o += 2 * LANES
    qkv_ref[...] = proj(o, o + qkv_w)
    z_ref[...] = proj(o + qkv_w, o + qkv_w + z_w)

    cos, sin_a, sin_b = cos_ref[...], sina_ref[...], sinb_ref[...]

    def rope(x):
        return (x * cos + pltpu.roll(x, LANES - half_rope, axis=1) * sin_a
                + pltpu.roll(x, half_rope, axis=1) * sin_b)

    ckv_ref[...] = ckv
    kpe = rope(kr)
    kpe_ref[...] = kpe

    cqn = _rms(cq, qn_ref[...]).astype(BF16)
    ckvb = ckv.astype(BF16)
    q_lin = _dot(cqn, wuq_ref[...])
    k_nope = _dot(ckvb, wuk_ref[...])
    vr = _vt_rows(v_dim)
    vt = _dot_nt(wuvt_ref[...], ckvb)
    ones_row = jnp.where(lax.broadcasted_iota(jnp.int32, (vr, 1), 0) == v_dim, 1.0, 0.0)
    for h in range(n_heads):
        sl = slice(h * LANES, (h + 1) * LANES)
        q_ref[h] = (rope(q_lin[:, sl]) * q_scale).astype(BF16)
        k_ref[h] = (k_nope[:, sl] + kpe).astype(BF16)
        vt_ref[h] = (vt[h * vr:(h + 1) * vr, :] + ones_row).astype(BF16)


def _front(x, tables, lw, l, dims):
    r, d = x.shape
    tm = ROW_TILE
    nh = dims["n_heads"]
    row = lambda w: pl.BlockSpec((tm, w), lambda i: (i, 0))
    wspec = lambda a: _resident((None,) + a.shape[1:], lambda i: (l,) + (0,) * (a.ndim - 1))
    out_shape = (
        jax.ShapeDtypeStruct((nh, r, LANES), BF16),
        jax.ShapeDtypeStruct((nh, r, LANES), BF16),
        jax.ShapeDtypeStruct((nh, _vt_rows(dims["v_dim"]), r), BF16),
        jax.ShapeDtypeStruct((r, dims["kvl"]), F32),
        jax.ShapeDtypeStruct((r, LANES), F32),
        jax.ShapeDtypeStruct((r, dims["qkv_w"]), F32),
        jax.ShapeDtypeStruct((r, dims["z_w"]), F32),
        jax.ShapeDtypeStruct((r, LANES), F32),
    )
    out_specs = (
        pl.BlockSpec((nh, tm, LANES), lambda i: (0, i, 0)),
        pl.BlockSpec((nh, tm, LANES), lambda i: (0, i, 0)),
        pl.BlockSpec((nh, _vt_rows(dims["v_dim"]), tm), lambda i: (0, 0, i)),
        row(dims["kvl"]), row(LANES), row(dims["qkv_w"]), row(dims["z_w"]), row(LANES),
    )
    weights = [lw["pre_mix_norm"], lw["w_in"], lw["q_norm"], lw["kv_norm"], lw["w_uq"], lw["w_uk"], lw["w_uvt"]]
    kern = functools.partial(_front_kernel, ql=dims["ql"], kvl=dims["kvl"], qkv_w=dims["qkv_w"],
                             z_w=dims["z_w"], n_heads=nh, v_dim=dims["v_dim"], half_rope=dims["rope"] // 2,
                             q_scale=dims["q_scale"])
    return pl.pallas_call(
        kern, out_shape=out_shape, grid=(r // tm,),
        in_specs=[row(d)] + [row(LANES)] * len(tables) + [wspec(w) for w in weights],
        out_specs=out_specs, compiler_params=_params(1), name=f"front_l{l}",
    )(x, *tables, *weights)


def _attn_kernel(q_ref, k_ref, vt_ref, o_ref, acc_ref, m_ref, s_ref, *, tile, fr0, n_meta, v_dim):
    sup = pl.program_id(1)
    is_frame = sup >= 1
    streams = [(hh, c) for hh in range(2) for c in range(2)]
    q = {(hh, c): q_ref[hh, c * tile:(c + 1) * tile, :] for hh, c in streams}

    def key_off(j):
        return pl.multiple_of(fr0 + j * tile, tile)

    def scores(hh, c, off):
        return _dot_nt(k_ref[hh, pl.ds(off, tile), :], q[hh, c])

    def update(hh, c, s, off, masked):
        if masked:
            kv_c = lax.broadcasted_iota(jnp.int32, s.shape, 0) // CHUNK
            q_c = lax.broadcasted_iota(jnp.int32, s.shape, 1) // CHUNK
            s = jnp.where(kv_c <= q_c, s, -jnp.inf)
        m_prev = m_ref[hh, c]
        m_new = jnp.maximum(m_prev, jnp.max(s, axis=0, keepdims=True))
        alpha = jnp.exp2(m_prev - m_new)
        p = jnp.exp2(s - m_new).astype(BF16)
        acc_ref[hh, c] = alpha * acc_ref[hh, c] + _dot(vt_ref[hh, :, pl.ds(off, tile)], p)
        m_ref[hh, c] = m_new

    mb = LANES
    s_meta = [_dot_nt(k_ref[hh, fr0 - mb:fr0, :], q[hh, c]) for hh, c in streams]
    for n, (hh, c) in enumerate(streams):
        s_ref[0, n] = scores(hh, c, key_off(0))
    for (hh, c), s in zip(streams, s_meta):
        rid = lax.broadcasted_iota(jnp.int32, s.shape, 0)
        s = jnp.where(rid >= mb - n_meta, s, -jnp.inf)
        m0 = jnp.max(s, axis=0, keepdims=True)
        m_ref[hh, c] = m0
        acc_ref[hh, c] = _dot(vt_ref[hh, :, fr0 - mb:fr0], jnp.exp2(s - m0).astype(BF16))

    def half(j_cur, slot_cur, slot_next):
        off_c, off_n = key_off(j_cur), key_off(j_cur + 1)
        for n, (hh, c) in enumerate(streams):
            s_ref[slot_next, n] = scores(hh, c, off_n)
            update(hh, c, s_ref[slot_cur, n], off_c, False)

    def body(t, carry):
        half(2 * t, 0, 1)
        half(2 * t + 1, 1, 0)
        return carry

    first = 2 * (sup - 1)
    lax.fori_loop(0, jnp.where(is_frame, sup - 1, 0), body, 0)

    @pl.when(is_frame)
    def _():
        off_a, off_b = key_off(first), key_off(first + 1)
        s_b = [scores(hh, 1, off_b) for hh in range(2)]
        for hh in range(2):
            update(hh, 0, s_ref[0, 2 * hh], off_a, True)
            update(hh, 1, s_ref[0, 2 * hh + 1], off_a, False)
        for hh in range(2):
            update(hh, 1, s_b[hh], off_b, True)

    for c in range(2):
        o = jnp.concatenate([acc_ref[hh, c, 0:v_dim, :] / acc_ref[hh, c, v_dim:v_dim + 1, :]
                             for hh in range(2)], axis=0)
        o_ref[c * tile:(c + 1) * tile, :] = o.T.astype(o_ref.dtype)


def _attention(q, k, vt, dims):
    nh, r, _ = q.shape
    tile = ROW_TILE
    v_dim = dims["v_dim"]
    vr = _vt_rows(v_dim)
    kern = functools.partial(_attn_kernel, tile=tile, fr0=dims["fr0"], n_meta=dims["n_meta"], v_dim=v_dim)
    return pl.pallas_call(
        kern, out_shape=jax.ShapeDtypeStruct((r, nh * v_dim), BF16),
        grid=(nh // 2, r // (2 * tile)),
        in_specs=[
            pl.BlockSpec((2, 2 * tile, LANES), lambda p, i: (p, i, 0)),
            _resident((2, r, LANES), lambda p, i: (p, 0, 0)),
            _resident((2, vr, r), lambda p, i: (p, 0, 0)),
        ],
        out_specs=pl.BlockSpec((2 * tile, 2 * v_dim), lambda p, i: (i, p)),
        scratch_shapes=[pltpu.VMEM((2, 2, vr, tile), F32), pltpu.VMEM((2, 2, 1, tile), F32),
                        pltpu.VMEM((2, 4, tile, tile), F32)],
        compiler_params=_params(2), name="prompt_attention",
    )(q, k, vt)


SATTN_BATCH = 2


def _sattn_kernel(q_ref, ckv_ref, kpe_ref, clat_ref, cpet_ref, wukt_ref, wuv_ref, o_in_ref, o_ref,
                  *, n_heads, nope, rope):
    del o_in_ref
    t = DEC_CHUNK
    staged = []
    for b in range(clat_ref.shape[0]):
        rows = slice(b * t, (b + 1) * t)
        clat = clat_ref[b].astype(BF16)
        cpet = cpet_ref[b].astype(BF16)
        ckv = ckv_ref[rows, :].astype(BF16)
        kpe = kpe_ref[rows, :].astype(BF16)
        qs = [q_ref[h, rows, :] for h in range(n_heads)]
        q_all = jnp.concatenate(qs, axis=0)
        q_lat = jnp.concatenate([_dot(qs[h][:, :nope], wukt_ref[h]) for h in range(n_heads)],
                                axis=0).astype(BF16)
        q_pe = q_all.astype(F32)[:, nope:nope + rope].astype(BF16)
        s_past = _dot_nt(q_lat, clat) + _dot(q_pe, cpet)
        s_new = _dot_nt(q_lat, ckv) + _dot_nt(q_all, kpe)
        staged.append((rows, clat, ckv, s_past, s_new))
    for rows, clat, ckv, s_past, s_new in staged:
        m = jnp.maximum(jnp.max(s_past, axis=-1, keepdims=True), jnp.max(s_new, axis=-1, keepdims=True))
        p_past = jnp.exp2(s_past - m)
        p_new = jnp.exp2(s_new - m)
        den = jnp.sum(p_past, axis=-1, keepdims=True) + jnp.sum(p_new, axis=-1, keepdims=True)
        o_lat = ((_dot(p_past.astype(BF16), clat) + _dot(p_new.astype(BF16), ckv)) / den).astype(BF16)
        o = jnp.concatenate([_dot(o_lat[h * t:(h + 1) * t], wuv_ref[h]) for h in range(n_heads)], axis=1)
        o_ref[rows, :] = o.astype(o_ref.dtype)


def _sample_attention(q, ckv, kpe, cache_lat, cache_pet, lw, l, o_mla, dims):
    nh = dims["n_heads"]
    sb = SATTN_BATCH
    t = sb * DEC_CHUNK
    nb = cache_lat.shape[1]
    past, kvl = cache_lat.shape[2:]
    rope = cache_pet.shape[2]
    kern = functools.partial(_sattn_kernel, n_heads=nh, nope=dims["nope"], rope=rope)
    return pl.pallas_call(
        kern, out_shape=jax.ShapeDtypeStruct(o_mla.shape, o_mla.dtype), grid=(nb // sb,),
        in_specs=[
            pl.BlockSpec((nh, t, LANES), lambda b: (0, b, 0)),
            pl.BlockSpec((t, kvl), lambda b: (b, 0)),
            pl.BlockSpec((t, LANES), lambda b: (b, 0)),
            pl.BlockSpec((None, sb, past, kvl), lambda b: (l, b, 0, 0)),
            pl.BlockSpec((None, sb, rope, past), lambda b: (l, b, 0, 0)),
            _resident((None,) + lw["w_uk_t"].shape[1:], lambda b: (l, 0, 0, 0)),
            _resident((None,) + lw["w_uv_h"].shape[1:], lambda b: (l, 0, 0, 0)),
            pl.BlockSpec(memory_space=pl.ANY),
        ],
        out_specs=pl.BlockSpec((t, o_mla.shape[1]), lambda b: (b, 0)),
        input_output_aliases={7: 0},
        compiler_params=_params(1), name=f"sample_attention_l{l}",
    )(q, ckv, kpe, cache_lat, cache_pet, lw["w_uk_t"], lw["w_uv_h"], o_mla)


def _split3(x):
    a = x.astype(BF16)
    r1 = x - a.astype(F32)
    b = r1.astype(BF16)
    return a, b, (r1 - b.astype(F32)).astype(BF16)


def _stack_heads(u, base, n_heads):
    return jnp.concatenate([u[:, base + h * LANES: base + (h + 1) * LANES] for h in range(n_heads)], axis=0)


def _gdn_prep(us, bas, alog, dtb, nh, c, consts, out):
    rows = bas[0].shape[0]
    nblk = rows // c
    dk = LANES
    n = nh * rows
    tri, incl, strict = consts
    idx = range(len(us))

    def unit(x):
        return x * lax.rsqrt(jnp.sum(x * x, axis=-1, keepdims=True) + EPS)

    qn = [unit(u[0]) * (dk ** -0.5) for u in us]
    kn = [unit(u[1]) for u in us]
    yield
    lane = lax.broadcasted_iota(jnp.int32, bas[0].shape, 1)
    g_lane = (lane >= nh) & (lane < 2 * nh)
    beta, gc = [], []
    for ba in bas:
        beta.append(jax.nn.sigmoid(ba))
        x = ba + dtb
        softplus = jnp.maximum(x, 0.0) + jnp.log1p(jnp.exp(-jnp.abs(x)))
        g1, g2, g3 = _split3(jnp.where(g_lane, -jnp.exp(alog) * softplus, 0.0))
        gc.append(_dot(tri, g1) + _dot(tri, g2) + _dot(tri, g3))
    knb = [x.astype(BF16) for x in kn]
    kk = [_dot_nt(x, x) for x in knb]
    qk = [_dot_nt(qn[i].astype(BF16), knb[i]) for i in idx]
    yield

    def col(a, lane0):
        return jnp.concatenate([jnp.broadcast_to(a[:, lane0 + h: lane0 + h + 1], (rows, LANES))
                                for h in range(nh)], axis=0)

    b_st = [col(x, 0) for x in beta]
    g_st = [col(x, nh) for x in gc]
    gl_st = [jnp.concatenate(
        [jnp.broadcast_to(x[h * rows + (s + 1) * c - 1: h * rows + (s + 1) * c], (c, LANES))
         for h in range(nh) for s in range(nblk)], axis=0) for x in g_st]
    eg = [jnp.exp(x) for x in g_st]
    yield
    decay = [jnp.exp(jnp.where(incl, jnp.concatenate([x] * (n // LANES), axis=1)
                               - jnp.broadcast_to(x.T[0:1, :], (n, n)), -jnp.inf)) for x in g_st]
    yield
    a0 = [jnp.where(strict, -(jnp.concatenate([b_st[i]] * (n // LANES), axis=1) * kk[i] * decay[i]), 0.0)
          for i in idx]
    yield
    for i in idx:
        v_st = us[i][2]
        out.append(dict(
            a=a0[i],
            vk=jnp.concatenate([v_st * b_st[i], kn[i] * b_st[i] * eg[i]], axis=1).astype(BF16),
            aqk=(qk[i] * decay[i]).astype(BF16), qg=(qn[i] * eg[i]).astype(BF16),
            kd=(kn[i] * jnp.exp(gl_st[i] - g_st[i])).astype(BF16), g_st=g_st[i]))
    yield


def _gdn_chunks(us, bas, zs, alog, dtb, gnw, states, c):
    rows = bas[0].shape[0]
    nh = len(states[0])
    nblk = rows // c
    n = nh * rows
    rr = lax.broadcasted_iota(jnp.int32, (rows, rows), 0)
    cc = lax.broadcasted_iota(jnp.int32, (rows, rows), 1)
    tri = jnp.where(((rr // c) == (cc // c)) & (rr >= cc), 1.0, 0.0).astype(BF16)
    r2 = lax.broadcasted_iota(jnp.int32, (n, n), 0)
    c2 = lax.broadcasted_iota(jnp.int32, (n, n), 1)
    same = (r2 // c) == (c2 // c)
    consts = (tri, same & (r2 >= c2), same & (r2 > c2))

    blocks = [(h, s) for h in range(nh) for s in range(nblk)]

    head_of_lane = lax.broadcasted_iota(jnp.int32, (rows, n), 1) // rows
    same_head = (r2 // rows) == (c2 // rows)
    eye_c = jnp.where(lax.broadcasted_iota(jnp.int32, (rows, n), 0)
                      == lax.broadcasted_iota(jnp.int32, (rows, n), 1) % rows, 1.0, 0.0)

    def compress(x):
        parts = [jnp.where(head_of_lane == h, x[h * rows:(h + 1) * rows, :], 0.0) for h in range(nh)]
        return functools.reduce(lambda a, b: a + b, parts)

    def expand(x_c):
        return jnp.where(same_head, jnp.concatenate([x_c] * nh, axis=0), jnp.zeros((), x_c.dtype))

    def solve(pre, uw):
        a_c = [compress(p["a"]) for p in pre]
        t_c = [eye_c + a for a in a_c]
        a_bf = [a.astype(BF16) for a in a_c]
        a_bd = [expand(a) for a in a_bf]
        for _ in range(int(math.log2(c)) - 1):
            a_bf = [_dot(a, bd).astype(BF16) for a, bd in zip(a_bf, a_bd)]
            yield
            a_bd = [expand(a) for a in a_bf]
            t_c = [t + _dot(t.astype(BF16), bd) for t, bd in zip(t_c, a_bd)]
            yield
        uw.extend(_dot(expand(t.astype(BF16)), p["vk"]) for t, p in zip(t_c, pre))
        yield

    box = [states]
    outs = []

    def scan(pre, uw, zs_g):
        for p, uw_c, z in zip(pre, uw, zs_g):
            st_in = box[0]
            u_st, w_st = uw_c[:, :LANES], uw_c[:, LANES:]
            ws_qs = [_dot(jnp.concatenate([w_st[h * rows + s * c: h * rows + (s + 1) * c].astype(BF16),
                                           p["qg"][h * rows + s * c: h * rows + (s + 1) * c]], axis=0),
                          st_in[s][h].astype(BF16)) for h, s in blocks]
            yield
            v_new = [u_st[h * rows + s * c: h * rows + (s + 1) * c] - x[:c] for (h, s), x in zip(blocks, ws_qs)]
            st_out = [[None] * nh for _ in range(nblk)]
            for (h, s), vn in zip(blocks, v_new):
                r0 = h * rows + s * c
                g_last = jnp.exp(p["g_st"][r0 + c - 1: r0 + c])
                st_out[s][h] = st_in[s][h] * g_last + _dot_tn(p["kd"][r0:r0 + c], vn.astype(BF16))
            box[0] = st_out
            yield
            o_st = (jnp.concatenate([x[c:] for x in ws_qs], axis=0)
                    + _dot(p["aqk"], jnp.concatenate(v_new, axis=0).astype(BF16)))
            o_n = _rms(o_st, gnw)
            o = jnp.concatenate([o_n[h * rows:(h + 1) * rows] for h in range(nh)], axis=1)
            outs.append(o * _silu(z))
            yield

    def drive(*gens):
        gens = list(gens)
        while gens:
            for g in list(gens):
                try:
                    next(g)
                except StopIteration:
                    gens.remove(g)

    groups = [list(range(i, min(i + GDN_GROUP, len(us)))) for i in range(0, len(us), GDN_GROUP)]
    prep = lambda g, out: _gdn_prep([us[i] for i in g], [bas[i] for i in g], alog, dtb, nh, c, consts, out)
    prev = None
    pre_next = []
    drive(prep(groups[0], pre_next))
    for gi, g in enumerate(groups):
        pre, pre_next, uw = pre_next, [], []
        gens = [solve(pre, uw)]
        if prev:
            gens.append(scan(*prev))
        if gi + 1 < len(groups):
            gens.append(prep(groups[gi + 1], pre_next))
        drive(*gens)
        prev = (pre, uw, [zs[i] for i in g])
    drive(scan(*prev))
    return outs, box[0]


def _conv_silu(windows, cw):
    y = windows[0] * cw[0:1]
    for i in range(1, len(windows)):
        y = y + windows[i] * cw[i:i + 1]
    return _silu(y)


def _conv_silu_strided(qkv_ref, keep, cw, xe_ref, u_ref, conv_w):
    nslab, ext, _ = xe_ref.shape
    rows = ext - SUBLANES
    pitch = ext // SUBLANES
    for c in range(nslab):
        lanes = slice(c * LANES, (c + 1) * LANES)
        xe_ref[c, SUBLANES:ext, :] = jnp.where(keep, qkv_ref[:, lanes], 0.0)
        x = [xe_ref[c, pl.ds(j, SUBLANES, stride=pitch), :] for j in range(pitch)]
        before = [pltpu.roll(x[pitch - k], 1, axis=0) for k in range(1, conv_w)]
        w = [cw[i:i + 1, lanes] for i in range(conv_w)]
        for j in range(pitch):
            taps = [x[j - d] if j >= d else before[d - j - 1] for d in range(conv_w)]
            y = taps[conv_w - 1] * w[0]
            for i in range(1, conv_w):
                y = y + taps[conv_w - 1 - i] * w[i]
            u_ref[c, pl.ds(j, SUBLANES, stride=pitch), :] = _silu(y)
        xe_ref[c, 0:SUBLANES, :] = xe_ref[c, rows:ext, :]


def _gdn_prompt_kernel(qkv_ref, ba_ref, z_ref, cw_ref, alog_ref, dtb_ref, gnw_ref, o_ref, s_out_ref,
                       xe_ref, u_ref, s_ref, *, n_heads, conv_w, lead_rows):
    step = pl.program_id(0)
    rows = qkv_ref.shape[0]

    @pl.when(step == 0)
    def _():
        xe_ref[:, 0:SUBLANES, :] = jnp.zeros((xe_ref.shape[0], SUBLANES, LANES), F32)
        s_ref[...] = jnp.zeros(s_ref.shape, F32)

    lead_steps = lead_rows // rows

    @pl.when(step < lead_steps)
    def _():
        o_ref[...] = jnp.zeros(o_ref.shape, o_ref.dtype)

    @pl.when(step >= lead_steps)
    def _():
        row = step * rows + lax.broadcasted_iota(jnp.int32, (rows, 1), 0)
        _conv_silu_strided(qkv_ref, row >= lead_rows, cw_ref[...], xe_ref, u_ref, conv_w)

        chunks = [slice(ck * CHUNK, (ck + 1) * CHUNK) for ck in range(rows // CHUNK)]

        def stacked(sl, part):
            return jnp.concatenate([u_ref[part * n_heads + h, SUBLANES + sl.start:SUBLANES + sl.stop, :]
                                    for h in range(n_heads)], axis=0)

        outs, states = _gdn_chunks([tuple(stacked(sl, part) for part in range(3)) for sl in chunks],
                                   [ba_ref[sl, :] for sl in chunks],
                                   [z_ref[sl, :] for sl in chunks], alog_ref[...], dtb_ref[...], gnw_ref[...],
                                   [[s_ref[h] for h in range(n_heads)]], CHUNK)
        o_ref[...] = jnp.concatenate(outs, axis=0).astype(o_ref.dtype)
        for h in range(n_heads):
            s_ref[h] = states[0][h]

    @pl.when(step == pl.num_programs(0) - 1)
    def _():
        s_out_ref[...] = s_ref[...]


def _gdn_prompt(qkv, ba, z, lw, l, dims):
    nh = dims["gdn_heads"]
    rows = GDN_ROWS
    r = qkv.shape[0]
    st_shape = (nh, LANES, LANES)
    wspec = lambda a: _resident((None,) + a.shape[1:], lambda i: (l,) + (0,) * (a.ndim - 1))
    row = lambda w: pl.BlockSpec((rows, w), lambda i: (i, 0))
    kern = functools.partial(_gdn_prompt_kernel, n_heads=nh, conv_w=dims["conv_w"],
                             lead_rows=dims["fr0"] - dims["n_meta"])
    params = [lw["conv_w"], lw["a_log"], lw["dt_bias"], lw["gdn_norm"]]
    return pl.pallas_call(
        kern,
        out_shape=(jax.ShapeDtypeStruct((r, z.shape[1]), BF16), jax.ShapeDtypeStruct(st_shape, F32)),
        grid=(r // rows,),
        in_specs=[row(qkv.shape[1]), row(LANES), row(z.shape[1])] + [wspec(p) for p in params],
        out_specs=(row(z.shape[1]), pl.BlockSpec(st_shape, lambda i: (0, 0, 0))),
        scratch_shapes=[pltpu.VMEM((qkv.shape[1] // LANES, rows + SUBLANES, LANES), F32),
                        pltpu.VMEM((qkv.shape[1] // LANES, rows + SUBLANES, LANES), F32),
                        pltpu.VMEM(st_shape, F32)],
        compiler_params=_params(1), name=f"gdn_prompt_l{l}",
    )(qkv, ba, z, *params)


def _gdn_sample_kernel(xe_ref, ba_ref, z_ref, cw_ref, alog_ref, dtb_ref, gnw_ref, s_in_ref, o_in_ref,
                       o_ref, s_out_ref, *, n_heads, conv_w):
    del o_in_ref
    nseq = xe_ref.shape[0]
    t = DEC_CHUNK
    cw = cw_ref[...]
    u = jnp.concatenate([_conv_silu([xe_ref[s, i:i + t, :] for i in range(conv_w)], cw) for s in range(nseq)],
                        axis=0)
    states = [[s_in_ref[s, h] for h in range(n_heads)] for s in range(nseq)]
    dk = LANES
    qkv_st = tuple(_stack_heads(u, i * n_heads * dk, n_heads) for i in range(3))
    outs, states = _gdn_chunks([qkv_st], [ba_ref[...]], [z_ref[...]], alog_ref[...], dtb_ref[...], gnw_ref[...],
                               states, t)
    o_ref[...] = outs[0].astype(o_ref.dtype)
    for s in range(nseq):
        for h in range(n_heads):
            s_out_ref[s, h] = states[s][h]


def _gdn_sample(xe, ba, z, lw, l, s_in, o_gdn, dims):
    nh = dims["gdn_heads"]
    nseq = xe.shape[0]
    sb = CHUNK // DEC_CHUNK
    wspec = lambda a: _resident((None,) + a.shape[1:], lambda i: (l,) + (0,) * (a.ndim - 1))
    row = lambda w: pl.BlockSpec((CHUNK, w), lambda i: (i, 0))
    kern = functools.partial(_gdn_sample_kernel, n_heads=nh, conv_w=dims["conv_w"])
    params = [lw["conv_w"], lw["a_log"], lw["dt_bias"], lw["gdn_norm"]]
    st_shape = s_in.shape[1:]
    return pl.pallas_call(
        kern,
        out_shape=(jax.ShapeDtypeStruct(o_gdn.shape, o_gdn.dtype), jax.ShapeDtypeStruct(st_shape, F32)),
        grid=(nseq // sb,),
        in_specs=[pl.BlockSpec((sb,) + xe.shape[1:], lambda i: (i, 0, 0)), row(LANES), row(z.shape[1])]
                 + [wspec(p) for p in params]
                 + [pl.BlockSpec((None, sb) + st_shape[1:], lambda i: (l, i, 0, 0, 0)),
                    pl.BlockSpec(memory_space=pl.ANY)],
        out_specs=(row(o_gdn.shape[1]), pl.BlockSpec((sb,) + st_shape[1:], lambda i: (i, 0, 0, 0))),
        input_output_aliases={8: 0},
        compiler_params=_params(1), name=f"gdn_sample_l{l}",
    )(xe, ba, z, *params, s_in, o_gdn)


def _back_kernel(x_ref, om_ref, og_ref, wo_ref, pmn_ref, pfn_ref, wg_ref, wu_ref, wd_ref, pon_ref, y_ref,
                 *, ff_chunks):
    half = om_ref.shape[1]
    mix = _dot(om_ref[...], wo_ref[0:half, :]) + _dot(og_ref[...], wo_ref[half:, :])
    h = x_ref[...] + _rms(mix, pmn_ref[...])
    hn = _rms(h, pfn_ref[...]).astype(BF16)
    f = None
    for c0, c1 in ff_chunks:
        act = (_silu(_dot(hn, wg_ref[:, c0:c1])) * _dot(hn, wu_ref[:, c0:c1])).astype(BF16)
        part = _dot(act, wd_ref[c0:c1, :])
        f = part if f is None else f + part
    y_ref[...] = h + _rms(f, pon_ref[...])


def _back(x, o_mla, o_gdn, lw, l):
    r, d = x.shape
    tm = ROW_TILE
    d_ff = lw["w_gate"].shape[-1]
    n_tiles = d_ff // 256
    cut = 256 * ((n_tiles + 1) // 2)
    ff_chunks = ((0, cut), (cut, d_ff)) if cut < d_ff else ((0, d_ff),)
    row = lambda w: pl.BlockSpec((tm, w), lambda i: (i, 0))
    wspec = lambda a: _resident((None,) + a.shape[1:], lambda i: (l,) + (0,) * (a.ndim - 1))
    weights = [lw["w_o"], lw["post_mix_norm"], lw["pre_ffn_norm"], lw["w_gate"], lw["w_up"], lw["w_down"],
               lw["post_ffn_norm"]]
    return pl.pallas_call(
        functools.partial(_back_kernel, ff_chunks=ff_chunks),
        out_shape=jax.ShapeDtypeStruct((r, d), F32), grid=(r // tm,),
        in_specs=[row(d), row(o_mla.shape[1]), row(o_gdn.shape[1])] + [wspec(w) for w in weights],
        out_specs=row(d), compiler_params=_params(1), name=f"back_l{l}",
    )(x, o_mla, o_gdn, *weights)


def _prepare_weights(pre_mix_norm, w_in, q_norm, kv_norm, w_uq, w_uk, w_uv, conv_w, a_log, dt_bias,
                     gdn_norm, w_o, post_mix_norm, pre_ffn_norm, w_gate, w_up, w_down, post_ffn_norm, dims):
    depth = w_in.shape[0]
    ql, kvl, rope, nope, nh = dims["ql"], dims["kvl"], dims["rope"], dims["nope"], dims["n_heads"]
    gh = dims["gdn_heads"]
    pad = LANES - nope - rope
    zeros = lambda *s: jnp.zeros((depth,) + s, F32)
    d = w_in.shape[1]

    o_pe = ql + kvl
    o_qkv = o_pe + rope
    o_z = o_qkv + dims["qkv_w"]
    o_b = o_z + dims["z_w"]
    kr_seg = jnp.concatenate([zeros(d, nope), w_in[..., o_pe:o_qkv], zeros(d, pad)], axis=-1)
    ba_seg = jnp.concatenate([w_in[..., o_b:o_b + 2 * gh], zeros(d, LANES - 2 * gh)], axis=-1)
    w_in_r = jnp.concatenate([w_in[..., :o_pe], kr_seg, ba_seg, w_in[..., o_qkv:o_b]], axis=-1)

    uq = w_uq.reshape(depth, ql, nh, nope + rope)
    w_uq_p = jnp.concatenate([uq, zeros(ql, nh, pad)], axis=-1).reshape(depth, ql, nh * LANES)
    w_uk_p = jnp.concatenate([w_uk, zeros(kvl, nh, LANES - nope)], axis=-1).reshape(depth, kvl, nh * LANES)
    vh = w_uv.shape[-1]
    vr = _vt_rows(vh)
    w_uv_p = jnp.concatenate([w_uv, zeros(kvl, nh, vr - vh)], axis=-1).reshape(depth, kvl, nh * vr)

    lane_vec = lambda a: jnp.concatenate([zeros(gh), a, zeros(LANES - 2 * gh)], axis=-1)[:, None, :]
    vec = lambda a: a[:, None, :].astype(F32)
    bf = lambda a: a.astype(BF16)
    return {
        "pre_mix_norm": vec(pre_mix_norm), "w_in": bf(w_in_r), "q_norm": vec(q_norm), "kv_norm": vec(kv_norm),
        "w_uq": bf(w_uq_p), "w_uk": bf(w_uk_p),
        "w_uvt": bf(jnp.swapaxes(w_uv_p, 1, 2)),
        "w_uk_t": bf(jnp.transpose(w_uk, (0, 2, 3, 1))), "w_uv_h": bf(jnp.transpose(w_uv, (0, 2, 1, 3))),
        "conv_w": conv_w.astype(F32), "a_log": lane_vec(a_log), "dt_bias": lane_vec(dt_bias),
        "gdn_norm": vec(gdn_norm), "w_o": bf(w_o), "post_mix_norm": vec(post_mix_norm),
        "pre_ffn_norm": vec(pre_ffn_norm), "w_gate": bf(w_gate), "w_up": bf(w_up), "w_down": bf(w_down),
        "post_ffn_norm": vec(post_ffn_norm),
    }


def _rope_tables(pos, dims):
    nope, rope = dims["nope"], dims["rope"]
    half = rope // 2
    inv = ROPE_THETA ** (-jnp.arange(half, dtype=F32) / half)
    ang = pos[:, None] * inv[None, :]
    c, s = jnp.cos(ang), jnp.sin(ang)
    n = pos.shape[0]
    z = lambda w: jnp.zeros((n, w), F32)
    pad = LANES - nope - rope
    cos_t = jnp.concatenate([jnp.ones((n, nope), F32), c, c, z(pad)], axis=-1)
    sin_a = jnp.concatenate([z(nope), -s, z(half + pad)], axis=-1)
    sin_b = jnp.concatenate([z(nope + half), s, z(pad)], axis=-1)
    return cos_t, sin_a, sin_b


def kernel(x_prompt, x_sample, cache_mla_latent, cache_mla_krope, state_gdn, state_gdn_conv, meta_tokens,
           pre_mix_norm, w_in, q_norm, kv_norm, w_uq, w_uk, w_uv, conv_w, a_log, dt_bias, gdn_norm, w_o,
           post_mix_norm, pre_ffn_norm, w_gate, w_up, w_down, post_ffn_norm):
    batch, seq, d = x_prompt.shape
    nb, t, _ = x_sample.shape
    depth, _, past, kvl = cache_mla_latent.shape
    n_meta = meta_tokens.shape[0]
    nh, nope = w_uk.shape[2], w_uk.shape[3]
    rope = cache_mla_krope.shape[-1]
    gh, dk, dv = state_gdn.shape[2:]
    cw = conv_w.shape[1]
    sm = nb * t
    fr0 = sm + ROW_TILE
    mt0 = fr0 - n_meta
    r = fr0 + seq
    assert batch == 1 and t == DEC_CHUNK and seq % (2 * ROW_TILE) == 0 and fr0 % (2 * ROW_TILE) == 0
    assert n_meta <= CHUNK and nh % 2 == 0 and nope + rope <= LANES and 2 * w_uv.shape[3] == LANES
    assert dk == LANES and dv == LANES and gh * CHUNK == GDN_STACK and cw - 1 <= SUBLANES and 2 * gh <= LANES
    assert sm % CHUNK == 0 and r % GDN_ROWS == 0 and n_meta + cw - 1 <= ROW_TILE
    assert nb % SATTN_BATCH == 0
    dims = dict(n_meta=n_meta, n_heads=nh, nope=nope, rope=rope, v_dim=w_uv.shape[3], ql=q_norm.shape[1],
                kvl=kvl, gdn_heads=gh, conv_w=cw, qkv_w=gh * (2 * dk + dv), z_w=gh * dv,
                q_scale=(nope + rope) ** -0.5 * math.log2(math.e), sm=sm, fr0=fr0)

    lw = _prepare_weights(pre_mix_norm, w_in, q_norm, kv_norm, w_uq, w_uk, w_uv, conv_w, a_log, dt_bias,
                          gdn_norm, w_o, post_mix_norm, pre_ffn_norm, w_gate, w_up, w_down, post_ffn_norm, dims)

    x = jnp.concatenate([x_sample.reshape(sm, d), jnp.zeros((ROW_TILE - n_meta, d), F32),
                         meta_tokens.astype(F32), x_prompt[0]], axis=0)
    pos = jnp.concatenate([jnp.tile(n_meta + past + jnp.arange(t), nb), jnp.zeros((ROW_TILE - n_meta,), jnp.int32),
                           jnp.arange(n_meta + seq)]).astype(F32)
    tables = _rope_tables(pos, dims)

    qkv_w = dims["qkv_w"]
    cache_pet = jnp.swapaxes(cache_mla_krope, 2, 3)
    outs = [[] for _ in range(8)]
    for l in range(depth):
        q, k, vt, ckv, kpe, qkv, z, ba = _front(x, tables, lw, l, dims)

        o_mla = _attention(q, k, vt, dims)
        o_mla = _sample_attention(q, ckv, kpe, cache_mla_latent, cache_pet, lw, l, o_mla, dims)

        o_gdn, s_prompt = _gdn_prompt(qkv, ba, z, lw, l, dims)
        xe_s = jnp.concatenate([state_gdn_conv[l].astype(F32), qkv[:sm].reshape(nb, t, qkv_w)], axis=1)
        o_gdn, s_sample = _gdn_sample(xe_s, ba, z, lw, l, state_gdn, o_gdn, dims)

        x = _back(x, o_mla, o_gdn, lw, l)

        kpe_r = kpe[:, nope:nope + rope]
        outs[0].append(ckv[mt0:][None])
        outs[1].append(kpe_r[mt0:][None])
        outs[2].append(s_prompt[None])
        outs[3].append(qkv[r - (cw - 1):][None])
        outs[4].append(ckv[:sm].reshape(nb, t, kvl))
        outs[5].append(kpe_r[:sm].reshape(nb, t, rope))
        outs[6].append(s_sample)
        outs[7].append(xe_s[:, t:])
    y_prompt = x[fr0:][None]
    y_sample = x[:sm].reshape(nb, t, d)
    return (y_prompt, y_sample) + tuple(jnp.stack(o) for o in outs)
```

```python
import functools
import math

import jax
import jax.numpy as jnp
from jax import lax
from jax.experimental import pallas as pl
from jax.experimental.pallas import tpu as pltpu

F32 = jnp.float32
BF16 = jnp.bfloat16

EPS = 1e-6
ROPE_THETA = 10000.0
CHUNK = 64
DEC_CHUNK = 16

LANES = 128
SUBLANES = 8
ROW_TILE = 512
GDN_ROWS = 1024
GDN_GROUP = 4
GDN_STACK = 256
VMEM_LIMIT = 56 * 1024 * 1024


def _dot(a, b):
    return jnp.dot(a, b, preferred_element_type=F32)


def _dot_nt(a, b):
    return lax.dot_general(a, b, (((1,), (1,)), ((), ())), preferred_element_type=F32)


def _dot_tn(a, b):
    return lax.dot_general(a, b, (((0,), (0,)), ((), ())), preferred_element_type=F32)


def _rms(x, w):
    return x * lax.rsqrt(jnp.mean(x * x, axis=-1, keepdims=True) + EPS) * w


def _silu(x):
    return x * jax.nn.sigmoid(x)


def _params(n_axes):
    return pltpu.CompilerParams(dimension_semantics=("arbitrary",) * n_axes,
                                vmem_limit_bytes=VMEM_LIMIT)


def _resident(shape, index_map):
    return pl.BlockSpec(shape, index_map, pipeline_mode=pl.Buffered(1))


def _vt_rows(v_dim):
    tile_rows = 2 * SUBLANES
    return -(-(v_dim + 1) // tile_rows) * tile_rows


def _front_kernel(x_ref, cos_ref, sina_ref, sinb_ref, prew_ref, win_ref, qn_ref, kvn_ref, wuq_ref,
                  wuk_ref, wuvt_ref, q_ref, k_ref, vt_ref, ckv_ref, kpe_ref, qkv_ref, z_ref, ba_ref,
                  *, ql, kvl, qkv_w, z_w, n_heads, v_dim, half_rope, q_scale):
    xb = _rms(x_ref[...], prew_ref[...]).astype(BF16)

    def proj(c0, c1):
        return _dot(xb, win_ref[:, c0:c1])

    o = ql + kvl
    cq = proj(0, ql)
    ckv = _rms(proj(ql, o), kvn_ref[...])
    kr = proj(o, o + LANES)
    ba_ref[...] = proj(o + LANES, o + 2 * LANES)
    o += 2 * LANES
    qkv_ref[...] = proj(o, o + qkv_w)
    z_ref[...] = proj(o + qkv_w, o + qkv_w + z_w)

    cos, sin_a, sin_b = cos_ref[...], sina_ref[...], sinb_ref[...]

    def rope(x):
        return (x * cos + pltpu.roll(x, LANES - half_rope, axis=1) * sin_a
                + pltpu.roll(x, half_rope, axis=1) * sin_b)

    ckv_ref[...] = ckv
    kpe = rope(kr)
    kpe_ref[...] = kpe

    cqn = _rms(cq, qn_ref[...]).astype(BF16)
    ckvb = ckv.astype(BF16)
    q_lin = _dot(cqn, wuq_ref[...])
    k_nope = _dot(ckvb, wuk_ref[...])
    vr = _vt_rows(v_dim)
    vt = _dot_nt(wuvt_ref[...], ckvb)
    ones_row = jnp.where(lax.broadcasted_iota(jnp.int32, (vr, 1), 0) == v_dim, 1.0, 0.0)
    for h in range(n_heads):
        sl = slice(h * LANES, (h + 1) * LANES)
        q_ref[h] = (rope(q_lin[:, sl]) * q_scale).astype(BF16)
        k_ref[h] = (k_nope[:, sl] + kpe).astype(BF16)
        vt_ref[h] = (vt[h * vr:(h + 1) * vr, :] + ones_row).astype(BF16)


def _front(x, tables, lw, l, dims):
    r, d = x.shape
    tm = ROW_TILE
    nh = dims["n_heads"]
    row = lambda w: pl.BlockSpec((tm, w), lambda i: (i, 0))
    wspec = lambda a: _resident((None,) + a.shape[1:], lambda i: (l,) + (0,) * (a.ndim - 1))
    out_shape = (
        jax.ShapeDtypeStruct((nh, r, LANES), BF16),
        jax.ShapeDtypeStruct((nh, r, LANES), BF16),
        jax.ShapeDtypeStruct((nh, _vt_rows(dims["v_dim"]), r), BF16),
        jax.ShapeDtypeStruct((r, dims["kvl"]), F32),
        jax.ShapeDtypeStruct((r, LANES), F32),
        jax.ShapeDtypeStruct((r, dims["qkv_w"]), F32),
        jax.ShapeDtypeStruct((r, dims["z_w"]), F32),
        jax.ShapeDtypeStruct((r, LANES), F32),
    )
    out_specs = (
        pl.BlockSpec((nh, tm, LANES), lambda i: (0, i, 0)),
        pl.BlockSpec((nh, tm, LANES), lambda i: (0, i, 0)),
        pl.BlockSpec((nh, _vt_rows(dims["v_dim"]), tm), lambda i: (0, 0, i)),
        row(dims["kvl"]), row(LANES), row(dims["qkv_w"]), row(dims["z_w"]), row(LANES),
    )
    weights = [lw["pre_mix_norm"], lw["w_in"], lw["q_norm"], lw["kv_norm"], lw["w_uq"], lw["w_uk"], lw["w_uvt"]]
    kern = functools.partial(_front_kernel, ql=dims["ql"], kvl=dims["kvl"], qkv_w=dims["qkv_w"],
                             z_w=dims["z_w"], n_heads=nh, v_dim=dims["v_dim"], half_rope=dims["rope"] // 2,
                             q_scale=dims["q_scale"])
    return pl.pallas_call(
        kern, out_shape=out_shape, grid=(r // tm,),
        in_specs=[row(d)] + [row(LANES)] * len(tables) + [wspec(w) for w in weights],
        out_specs=out_specs, compiler_params=_params(1), name=f"front_l{l}",
    )(x, *tables, *weights)


def _attn_kernel(q_ref, k_ref, vt_ref, o_ref, acc_ref, m_ref, s_ref, *, tile, fr0, n_meta, v_dim):
    sup = pl.program_id(1)
    is_frame = sup >= 1
    streams = [(hh, c) for hh in range(2) for c in range(2)]
    q = {(hh, c): q_ref[hh, c * tile:(c + 1) * tile, :] for hh, c in streams}

    def key_off(j):
        return pl.multiple_of(fr0 + j * tile, tile)

    def scores(hh, c, off):
        return _dot_nt(k_ref[hh, pl.ds(off, tile), :], q[hh, c])

    def update(hh, c, s, off, masked):
        if masked:
            kv_c = lax.broadcasted_iota(jnp.int32, s.shape, 0) // CHUNK
            q_c = lax.broadcasted_iota(jnp.int32, s.shape, 1) // CHUNK
            s = jnp.where(kv_c <= q_c, s, -jnp.inf)
        m_prev = m_ref[hh, c]
        m_new = jnp.maximum(m_prev, jnp.max(s, axis=0, keepdims=True))
        alpha = jnp.exp2(m_prev - m_new)
        p = jnp.exp2(s - m_new).astype(BF16)
        acc_ref[hh, c] = alpha * acc_ref[hh, c] + _dot(vt_ref[hh, :, pl.ds(off, tile)], p)
        m_ref[hh, c] = m_new

    mb = LANES
    s_meta = [_dot_nt(k_ref[hh, fr0 - mb:fr0, :], q[hh, c]) for hh, c in streams]
    for n, (hh, c) in enumerate(streams):
        s_ref[0, n] = scores(hh, c, key_off(0))
    for (hh, c), s in zip(streams, s_meta):
        rid = lax.broadcasted_iota(jnp.int32, s.shape, 0)
        s = jnp.where(rid >= mb - n_meta, s, -jnp.inf)
        m0 = jnp.max(s, axis=0, keepdims=True)
        m_ref[hh, c] = m0
        acc_ref[hh, c] = _dot(vt_ref[hh, :, fr0 - mb:fr0], jnp.exp2(s - m0).astype(BF16))

    def half(j_cur, slot_cur, slot_next):
        off_c, off_n = key_off(j_cur), key_off(j_cur + 1)
        for n, (hh, c) in enumerate(streams):
            s_ref[slot_next, n] = scores(hh, c, off_n)
            update(hh, c, s_ref[slot_cur, n], off_c, False)

    def body(t, carry):
        half(2 * t, 0, 1)
        half(2 * t + 1, 1, 0)
        return carry

    first = 2 * (sup - 1)
    lax.fori_loop(0, jnp.where(is_frame, sup - 1, 0), body, 0)

    @pl.when(is_frame)
    def _():
        off_a, off_b = key_off(first), key_off(first + 1)
        s_b = [scores(hh, 1, off_b) for hh in range(2)]
        for hh in range(2):
            update(hh, 0, s_ref[0, 2 * hh], off_a, True)
            update(hh, 1, s_ref[0, 2 * hh + 1], off_a, False)
        for hh in range(2):
            update(hh, 1, s_b[hh], off_b, True)

    for c in range(2):
        o = jnp.concatenate([acc_ref[hh, c, 0:v_dim, :] / acc_ref[hh, c, v_dim:v_dim + 1, :]
                             for hh in range(2)], axis=0)
        o_ref[c * tile:(c + 1) * tile, :] = o.T.astype(o_ref.dtype)


def _attention(q, k, vt, dims):
    nh, r, _ = q.shape
    tile = ROW_TILE
    v_dim = dims["v_dim"]
    vr = _vt_rows(v_dim)
    kern = functools.partial(_attn_kernel, tile=tile, fr0=dims["fr0"], n_meta=dims["n_meta"], v_dim=v_dim)
    return pl.pallas_call(
        kern, out_shape=jax.ShapeDtypeStruct((r, nh * v_dim), BF16),
        grid=(nh // 2, r // (2 * tile)),
        in_specs=[
            pl.BlockSpec((2, 2 * tile, LANES), lambda p, i: (p, i, 0)),
            _resident((2, r, LANES), lambda p, i: (p, 0, 0)),
            _resident((2, vr, r), lambda p, i: (p, 0, 0)),
        ],
        out_specs=pl.BlockSpec((2 * tile, 2 * v_dim), lambda p, i: (i, p)),
        scratch_shapes=[pltpu.VMEM((2, 2, vr, tile), F32), pltpu.VMEM((2, 2, 1, tile), F32),
                        pltpu.VMEM((2, 4, tile, tile), F32)],
        compiler_params=_params(2), name="prompt_attention",
    )(q, k, vt)


SATTN_BATCH = 2


def _sattn_kernel(q_ref, ckv_ref, kpe_ref, clat_ref, cpet_ref, wukt_ref, wuv_ref, o_in_ref, o_ref,
                  *, n_heads, nope, rope):
    del o_in_ref
    t = DEC_CHUNK
    staged = []
    for b in range(clat_ref.shape[0]):
        rows = slice(b * t, (b + 1) * t)
        clat = clat_ref[b].astype(BF16)
        cpet = cpet_ref[b].astype(BF16)
        ckv = ckv_ref[rows, :].astype(BF16)
        kpe = kpe_ref[rows, :].astype(BF16)
        qs = [q_ref[h, rows, :] for h in range(n_heads)]
        q_all = jnp.concatenate(qs, axis=0)
        q_lat = jnp.concatenate([_dot(qs[h][:, :nope], wukt_ref[h]) for h in range(n_heads)],
                                axis=0).astype(BF16)
        q_pe = q_all.astype(F32)[:, nope:nope + rope].astype(BF16)
        s_past = _dot_nt(q_lat, clat) + _dot(q_pe, cpet)
        s_new = _dot_nt(q_lat, ckv) + _dot_nt(q_all, kpe)
        staged.append((rows, clat, ckv, s_past, s_new))
    for rows, clat, ckv, s_past, s_new in staged:
        m = jnp.maximum(jnp.max(s_past, axis=-1, keepdims=True), jnp.max(s_new, axis=-1, keepdims=True))
        p_past = jnp.exp2(s_past - m)
        p_new = jnp.exp2(s_new - m)
        den = jnp.sum(p_past, axis=-1, keepdims=True) + jnp.sum(p_new, axis=-1, keepdims=True)
        o_lat = ((_dot(p_past.astype(BF16), clat) + _dot(p_new.astype(BF16), ckv)) / den).astype(BF16)
        o = jnp.concatenate([_dot(o_lat[h * t:(h + 1) * t], wuv_ref[h]) for h in range(n_heads)], axis=1)
        o_ref[rows, :] = o.astype(o_ref.dtype)


def _sample_attention(q, ckv, kpe, cache_lat, cache_pet, lw, l, o_mla, dims):
    nh = dims["n_heads"]
    sb = SATTN_BATCH
    t = sb * DEC_CHUNK
    nb = cache_lat.shape[1]
    past, kvl = cache_lat.shape[2:]
    rope = cache_pet.shape[2]
    kern = functools.partial(_sattn_kernel, n_heads=nh, nope=dims["nope"], rope=rope)
    return pl.pallas_call(
        kern, out_shape=jax.ShapeDtypeStruct(o_mla.shape, o_mla.dtype), grid=(nb // sb,),
        in_specs=[
            pl.BlockSpec((nh, t, LANES), lambda b: (0, b, 0)),
            pl.BlockSpec((t, kvl), lambda b: (b, 0)),
            pl.BlockSpec((t, LANES), lambda b: (b, 0)),
            pl.BlockSpec((None, sb, past, kvl), lambda b: (l, b, 0, 0)),
            pl.BlockSpec((None, sb, rope, past), lambda b: (l, b, 0, 0)),
            _resident((None,) + lw["w_uk_t"].shape[1:], lambda b: (l, 0, 0, 0)),
            _resident((None,) + lw["w_uv_h"].shape[1:], lambda b: (l, 0, 0, 0)),
            pl.BlockSpec(memory_space=pl.ANY),
        ],
        out_specs=pl.BlockSpec((t, o_mla.shape[1]), lambda b: (b, 0)),
        input_output_aliases={7: 0},
        compiler_params=_params(1), name=f"sample_attention_l{l}",
    )(q, ckv, kpe, cache_lat, cache_pet, lw["w_uk_t"], lw["w_uv_h"], o_mla)


def _split3(x):
    a = x.astype(BF16)
    r1 = x - a.astype(F32)
    b = r1.astype(BF16)
    return a, b, (r1 - b.astype(F32)).astype(BF16)


def _stack_heads(u, base, n_heads):
    return jnp.concatenate([u[:, base + h * LANES: base + (h + 1) * LANES] for h in range(n_heads)], axis=0)


def _gdn_prep(us, bas, alog, dtb, nh, c, consts, out):
    rows = bas[0].shape[0]
    nblk = rows // c
    dk = LANES
    n = nh * rows
    tri, incl, strict = consts
    idx = range(len(us))

    def unit(x):
        return x * lax.rsqrt(jnp.sum(x * x, axis=-1, keepdims=True) + EPS)

    qn = [unit(u[0]) * (dk ** -0.5) for u in us]
    kn = [unit(u[1]) for u in us]
    yield
    lane = lax.broadcasted_iota(jnp.int32, bas[0].shape, 1)
    g_lane = (lane >= nh) & (lane < 2 * nh)
    beta, gc = [], []
    for ba in bas:
        beta.append(jax.nn.sigmoid(ba))
        x = ba + dtb
        softplus = jnp.maximum(x, 0.0) + jnp.log1p(jnp.exp(-jnp.abs(x)))
        g1, g2, g3 = _split3(jnp.where(g_lane, -jnp.exp(alog) * softplus, 0.0))
        gc.append(_dot(tri, g1) + _dot(tri, g2) + _dot(tri, g3))
    knb = [x.astype(BF16) for x in kn]
    kk = [_dot_nt(x, x) for x in knb]
    qk = [_dot_nt(qn[i].astype(BF16), knb[i]) for i in idx]
    yield

    def col(a, lane0):
        return jnp.concatenate([jnp.broadcast_to(a[:, lane0 + h: lane0 + h + 1], (rows, LANES))
                                for h in range(nh)], axis=0)

    b_st = [col(x, 0) for x in beta]
    g_st = [col(x, nh) for x in gc]
    gl_st = [jnp.concatenate(
        [jnp.broadcast_to(x[h * rows + (s + 1) * c - 1: h * rows + (s + 1) * c], (c, LANES))
         for h in range(nh) for s in range(nblk)], axis=0) for x in g_st]
    eg = [jnp.exp(x) for x in g_st]
    yield
    decay = [jnp.exp(jnp.where(incl, jnp.concatenate([x] * (n // LANES), axis=1)
                               - jnp.broadcast_to(x.T[0:1, :], (n, n)), -jnp.inf)) for x in g_st]
    yield
    a0 = [jnp.where(strict, -(jnp.concatenate([b_st[i]] * (n // LANES), axis=1) * kk[i] * decay[i]), 0.0)
          for i in idx]
    yield
    for i in idx:
        v_st = us[i][2]
        out.append(dict(
            a=a0[i],
            vk=jnp.concatenate([v_st * b_st[i], kn[i] * b_st[i] * eg[i]], axis=1).astype(BF16),
            aqk=(qk[i] * decay[i]).astype(BF16), qg=(qn[i] * eg[i]).astype(BF16),
            kd=(kn[i] * jnp.exp(gl_st[i] - g_st[i])).astype(BF16), g_st=g_st[i]))
    yield


def _gdn_chunks(us, bas, zs, alog, dtb, gnw, states, c):
    rows = bas[0].shape[0]
    nh = len(states[0])
    nblk = rows // c
    n = nh * rows
    rr = lax.broadcasted_iota(jnp.int32, (rows, rows), 0)
    cc = lax.broadcasted_iota(jnp.int32, (rows, rows), 1)
    tri = jnp.where(((rr // c) == (cc // c)) & (rr >= cc), 1.0, 0.0).astype(BF16)
    r2 = lax.broadcasted_iota(jnp.int32, (n, n), 0)
    c2 = lax.broadcasted_iota(jnp.int32, (n, n), 1)
    same = (r2 // c) == (c2 // c)
    consts = (tri, same & (r2 >= c2), same & (r2 > c2))

    blocks = [(h, s) for h in range(nh) for s in range(nblk)]

    head_of_lane = lax.broadcasted_iota(jnp.int32, (rows, n), 1) // rows
    same_head = (r2 // rows) == (c2 // rows)
    eye_c = jnp.where(lax.broadcasted_iota(jnp.int32, (rows, n), 0)
                      == lax.broadcasted_iota(jnp.int32, (rows, n), 1) % rows, 1.0, 0.0)

    def compress(x):
        parts = [jnp.where(head_of_lane == h, x[h * rows:(h + 1) * rows, :], 0.0) for h in range(nh)]
        return functools.reduce(lambda a, b: a + b, parts)

    def expand(x_c):
        return jnp.where(same_head, jnp.concatenate([x_c] * nh, axis=0), jnp.zeros((), x_c.dtype))

    def solve(pre, uw):
        a_c = [compress(p["a"]) for p in pre]
        t_c = [eye_c + a for a in a_c]
        a_bf = [a.astype(BF16) for a in a_c]
        a_bd = [expand(a) for a in a_bf]
        for _ in range(int(math.log2(c)) - 1):
            a_bf = [_dot(a, bd).astype(BF16) for a, bd in zip(a_bf, a_bd)]
            yield
            a_bd = [expand(a) for a in a_bf]
            t_c = [t + _dot(t.astype(BF16), bd) for t, bd in zip(t_c, a_bd)]
            yield
        uw.extend(_dot(expand(t.astype(BF16)), p["vk"]) for t, p in zip(t_c, pre))
        yield

    box = [states]
    outs = []

    def scan(pre, uw, zs_g):
        for p, uw_c, z in zip(pre, uw, zs_g):
            st_in = box[0]
            u_st, w_st = uw_c[:, :LANES], uw_c[:, LANES:]
            ws_qs = [_dot(jnp.concatenate([w_st[h * rows + s * c: h * rows + (s + 1) * c].astype(BF16),
                                           p["qg"][h * rows + s * c: h * rows + (s + 1) * c]], axis=0),
                          st_in[s][h].astype(BF16)) for h, s in blocks]
            yield
            v_new = [u_st[h * rows + s * c: h * rows + (s + 1) * c] - x[:c] for (h, s), x in zip(blocks, ws_qs)]
            st_out = [[None] * nh for _ in range(nblk)]
            for (h, s), vn in zip(blocks, v_new):
                r0 = h * rows + s * c
                g_last = jnp.exp(p["g_st"][r0 + c - 1: r0 + c])
                st_out[s][h] = st_in[s][h] * g_last + _dot_tn(p["kd"][r0:r0 + c], vn.astype(BF16))
            box[0] = st_out
            yield
            o_st = (jnp.concatenate([x[c:] for x in ws_qs], axis=0)
                    + _dot(p["aqk"], jnp.concatenate(v_new, axis=0).astype(BF16)))
            o_n = _rms(o_st, gnw)
            o = jnp.concatenate([o_n[h * rows:(h + 1) * rows] for h in range(nh)], axis=1)
            outs.append(o * _silu(z))
            yield

    def drive(*gens):
        gens = list(gens)
        while gens:
            for g in list(gens):
                try:
                    next(g)
                except StopIteration:
                    gens.remove(g)

    groups = [list(range(i, min(i + GDN_GROUP, len(us)))) for i in range(0, len(us), GDN_GROUP)]
    prep = lambda g, out: _gdn_prep([us[i] for i in g], [bas[i] for i in g], alog, dtb, nh, c, consts, out)
    prev = None
    pre_next = []
    drive(prep(groups[0], pre_next))
    for gi, g in enumerate(groups):
        pre, pre_next, uw = pre_next, [], []
        gens = [solve(pre, uw)]
        if prev:
            gens.append(scan(*prev))
        if gi + 1 < len(groups):
            gens.append(prep(groups[gi + 1], pre_next))
        drive(*gens)
        prev = (pre, uw, [zs[i] for i in g])
    drive(scan(*prev))
    return outs, box[0]


def _conv_silu(windows, cw):
    y = windows[0] * cw[0:1]
    for i in range(1, len(windows)):
        y = y + windows[i] * cw[i:i + 1]
    return _silu(y)


def _conv_silu_strided(qkv_ref, keep, cw, xe_ref, u_ref, conv_w):
    nslab, ext, _ = xe_ref.shape
    rows = ext - SUBLANES
    pitch = ext // SUBLANES
    for c in range(nslab):
        lanes = slice(c * LANES, (c + 1) * LANES)
        xe_ref[c, SUBLANES:ext, :] = jnp.where(keep, qkv_ref[:, lanes], 0.0)
        x = [xe_ref[c, pl.ds(j, SUBLANES, stride=pitch), :] for j in range(pitch)]
        before = [pltpu.roll(x[pitch - k], 1, axis=0) for k in range(1, conv_w)]
        w = [cw[i:i + 1, lanes] for i in range(conv_w)]
        for j in range(pitch):
            taps = [x[j - d] if j >= d else before[d - j - 1] for d in range(conv_w)]
            y = taps[conv_w - 1] * w[0]
            for i in range(1, conv_w):
                y = y + taps[conv_w - 1 - i] * w[i]
            u_ref[c, pl.ds(j, SUBLANES, stride=pitch), :] = _silu(y)
        xe_ref[c, 0:SUBLANES, :] = xe_ref[c, rows:ext, :]


def _gdn_prompt_kernel(qkv_ref, ba_ref, z_ref, cw_ref, alog_ref, dtb_ref, gnw_ref, o_ref, s_out_ref,
                       xe_ref, u_ref, s_ref, *, n_heads, conv_w, lead_rows):
    step = pl.program_id(0)
    rows = qkv_ref.shape[0]

    @pl.when(step == 0)
    def _():
        xe_ref[:, 0:SUBLANES, :] = jnp.zeros((xe_ref.shape[0], SUBLANES, LANES), F32)
        s_ref[...] = jnp.zeros(s_ref.shape, F32)

    lead_steps = lead_rows // rows

    @pl.when(step < lead_steps)
    def _():
        o_ref[...] = jnp.zeros(o_ref.shape, o_ref.dtype)

    @pl.when(step >= lead_steps)
    def _():
        row = step * rows + lax.broadcasted_iota(jnp.int32, (rows, 1), 0)
        _conv_silu_strided(qkv_ref, row >= lead_rows, cw_ref[...], xe_ref, u_ref, conv_w)

        chunks = [slice(ck * CHUNK, (ck + 1) * CHUNK) for ck in range(rows // CHUNK)]

        def stacked(sl, part):
            return jnp.concatenate([u_ref[part * n_heads + h, SUBLANES + sl.start:SUBLANES + sl.stop, :]
                                    for h in range(n_heads)], axis=0)

        outs, states = _gdn_chunks([tuple(stacked(sl, part) for part in range(3)) for sl in chunks],
                                   [ba_ref[sl, :] for sl in chunks],
                                   [z_ref[sl, :] for sl in chunks], alog_ref[...], dtb_ref[...], gnw_ref[...],
                                   [[s_ref[h] for h in range(n_heads)]], CHUNK)
        o_ref[...] = jnp.concatenate(outs, axis=0).astype(o_ref.dtype)
        for h in range(n_heads):
            s_ref[h] = states[0][h]

    @pl.when(step == pl.num_programs(0) - 1)
    def _():
        s_out_ref[...] = s_ref[...]


def _gdn_prompt(qkv, ba, z, lw, l, dims):
    nh = dims["gdn_heads"]
    rows = GDN_ROWS
    r = qkv.shape[0]
    st_shape = (nh, LANES, LANES)
    wspec = lambda a: _resident((None,) + a.shape[1:], lambda i: (l,) + (0,) * (a.ndim - 1))
    row = lambda w: pl.BlockSpec((rows, w), lambda i: (i, 0))
    kern = functools.partial(_gdn_prompt_kernel, n_heads=nh, conv_w=dims["conv_w"],
                             lead_rows=dims["fr0"] - dims["n_meta"])
    params = [lw["conv_w"], lw["a_log"], lw["dt_bias"], lw["gdn_norm"]]
    return pl.pallas_call(
        kern,
        out_shape=(jax.ShapeDtypeStruct((r, z.shape[1]), BF16), jax.ShapeDtypeStruct(st_shape, F32)),
        grid=(r // rows,),
        in_specs=[row(qkv.shape[1]), row(LANES), row(z.shape[1])] + [wspec(p) for p in params],
        out_specs=(row(z.shape[1]), pl.BlockSpec(st_shape, lambda i: (0, 0, 0))),
        scratch_shapes=[pltpu.VMEM((qkv.shape[1] // LANES, rows + SUBLANES, LANES), F32),
                        pltpu.VMEM((qkv.shape[1] // LANES, rows + SUBLANES, LANES), F32),
                        pltpu.VMEM(st_shape, F32)],
        compiler_params=_params(1), name=f"gdn_prompt_l{l}",
    )(qkv, ba, z, *params)


def _gdn_sample_kernel(xe_ref, ba_ref, z_ref, cw_ref, alog_ref, dtb_ref, gnw_ref, s_in_ref, o_in_ref,
                       o_ref, s_out_ref, *, n_heads, conv_w):
    del o_in_ref
    nseq = xe_ref.shape[0]
    t = DEC_CHUNK
    cw = cw_ref[...]
    u = jnp.concatenate([_conv_silu([xe_ref[s, i:i + t, :] for i in range(conv_w)], cw) for s in range(nseq)],
                        axis=0)
    states = [[s_in_ref[s, h] for h in range(n_heads)] for s in range(nseq)]
    dk = LANES
    qkv_st = tuple(_stack_heads(u, i * n_heads * dk, n_heads) for i in range(3))
    outs, states = _gdn_chunks([qkv_st], [ba_ref[...]], [z_ref[...]], alog_ref[...], dtb_ref[...], gnw_ref[...],
                               states, t)
    o_ref[...] = outs[0].astype(o_ref.dtype)
    for s in range(nseq):
        for h in range(n_heads):
            s_out_ref[s, h] = states[s][h]


def _gdn_sample(xe, ba, z, lw, l, s_in, o_gdn, dims):
    nh = dims["gdn_heads"]
    nseq = xe.shape[0]
    sb = CHUNK // DEC_CHUNK
    wspec = lambda a: _resident((None,) + a.shape[1:], lambda i: (l,) + (0,) * (a.ndim - 1))
    row = lambda w: pl.BlockSpec((CHUNK, w), lambda i: (i, 0))
    kern = functools.partial(_gdn_sample_kernel, n_heads=nh, conv_w=dims["conv_w"])
    params = [lw["conv_w"], lw["a_log"], lw["dt_bias"], lw["gdn_norm"]]
    st_shape = s_in.shape[1:]
    return pl.pallas_call(
        kern,
        out_shape=(jax.ShapeDtypeStruct(o_gdn.shape, o_gdn.dtype), jax.ShapeDtypeStruct(st_shape, F32)),
        grid=(nseq // sb,),
        in_specs=[pl.BlockSpec((sb,) + xe.shape[1:], lambda i: (i, 0, 0)), row(LANES), row(z.shape[1])]
                 + [wspec(p) for p in params]
                 + [pl.BlockSpec((None, sb) + st_shape[1:], lambda i: (l, i, 0, 0, 0)),
                    pl.BlockSpec(memory_space=pl.ANY)],
        out_specs=(row(o_gdn.shape[1]), pl.BlockSpec((sb,) + st_shape[1:], lambda i: (i, 0, 0, 0))),
        input_output_aliases={8: 0},
        compiler_params=_params(1), name=f"gdn_sample_l{l}",
    )(xe, ba, z, *params, s_in, o_gdn)


def _back_kernel(x_ref, om_ref, og_ref, wo_ref, pmn_ref, pfn_ref, wg_ref, wu_ref, wd_ref, pon_ref, y_ref,
                 *, ff_chunks):
    half = om_ref.shape[1]
    mix = _dot(om_ref[...], wo_ref[0:half, :]) + _dot(og_ref[...], wo_ref[half:, :])
    h = x_ref[...] + _rms(mix, pmn_ref[...])
    hn = _rms(h, pfn_ref[...]).astype(BF16)
    f = None
    for c0, c1 in ff_chunks:
        act = (_silu(_dot(hn, wg_ref[:, c0:c1])) * _dot(hn, wu_ref[:, c0:c1])).astype(BF16)
        part = _dot(act, wd_ref[c0:c1, :])
        f = part if f is None else f + part
    y_ref[...] = h + _rms(f, pon_ref[...])


def _back(x, o_mla, o_gdn, lw, l):
    r, d = x.shape
    tm = ROW_TILE
    d_ff = lw["w_gate"].shape[-1]
    n_tiles = d_ff // 256
    cut = 256 * ((n_tiles + 1) // 2)
    ff_chunks = ((0, cut), (cut, d_ff)) if cut < d_ff else ((0, d_ff),)
    row = lambda w: pl.BlockSpec((tm, w), lambda i: (i, 0))
    wspec = lambda a: _resident((None,) + a.shape[1:], lambda i: (l,) + (0,) * (a.ndim - 1))
    weights = [lw["w_o"], lw["post_mix_norm"], lw["pre_ffn_norm"], lw["w_gate"], lw["w_up"], lw["w_down"],
               lw["post_ffn_norm"]]
    return pl.pallas_call(
        functools.partial(_back_kernel, ff_chunks=ff_chunks),
        out_shape=jax.ShapeDtypeStruct((r, d), F32), grid=(r // tm,),
        in_specs=[row(d), row(o_mla.shape[1]), row(o_gdn.shape[1])] + [wspec(w) for w in weights],
        out_specs=row(d), compiler_params=_params(1), name=f"back_l{l}",
    )(x, o_mla, o_gdn, *weights)


def _prepare_weights(pre_mix_norm, w_in, q_norm, kv_norm, w_uq, w_uk, w_uv, conv_w, a_log, dt_bias,
                     gdn_norm, w_o, post_mix_norm, pre_ffn_norm, w_gate, w_up, w_down, post_ffn_norm, dims):
    depth = w_in.shape[0]
    ql, kvl, rope, nope, nh = dims["ql"], dims["kvl"], dims["rope"], dims["nope"], dims["n_heads"]
    gh = dims["gdn_heads"]
    pad = LANES - nope - rope
    zeros = lambda *s: jnp.zeros((depth,) + s, F32)
    d = w_in.shape[1]

    o_pe = ql + kvl
    o_qkv = o_pe + rope
    o_z = o_qkv + dims["qkv_w"]
    o_b = o_z + dims["z_w"]
    kr_seg = jnp.concatenate([zeros(d, nope), w_in[..., o_pe:o_qkv], zeros(d, pad)], axis=-1)
    ba_seg = jnp.concatenate([w_in[..., o_b:o_b + 2 * gh], zeros(d, LANES - 2 * gh)], axis=-1)
    w_in_r = jnp.concatenate([w_in[..., :o_pe], kr_seg, ba_seg, w_in[..., o_qkv:o_b]], axis=-1)

    uq = w_uq.reshape(depth, ql, nh, nope + rope)
    w_uq_p = jnp.concatenate([uq, zeros(ql, nh, pad)], axis=-1).reshape(depth, ql, nh * LANES)
    w_uk_p = jnp.concatenate([w_uk, zeros(kvl, nh, LANES - nope)], axis=-1).reshape(depth, kvl, nh * LANES)
    vh = w_uv.shape[-1]
    vr = _vt_rows(vh)
    w_uv_p = jnp.concatenate([w_uv, zeros(kvl, nh, vr - vh)], axis=-1).reshape(depth, kvl, nh * vr)

    lane_vec = lambda a: jnp.concatenate([zeros(gh), a, zeros(LANES - 2 * gh)], axis=-1)[:, None, :]
    vec = lambda a: a[:, None, :].astype(F32)
    bf = lambda a: a.astype(BF16)
    return {
        "pre_mix_norm": vec(pre_mix_norm), "w_in": bf(w_in_r), "q_norm": vec(q_norm), "kv_norm": vec(kv_norm),
        "w_uq": bf(w_uq_p), "w_uk": bf(w_uk_p),
        "w_uvt": bf(jnp.swapaxes(w_uv_p, 1, 2)),
        "w_uk_t": bf(jnp.transpose(w_uk, (0, 2, 3, 1))), "w_uv_h": bf(jnp.transpose(w_uv, (0, 2, 1, 3))),
        "conv_w": conv_w.astype(F32), "a_log": lane_vec(a_log), "dt_bias": lane_vec(dt_bias),
        "gdn_norm": vec(gdn_norm), "w_o": bf(w_o), "post_mix_norm": vec(post_mix_norm),
        "pre_ffn_norm": vec(pre_ffn_norm), "w_gate": bf(w_gate), "w_up": bf(w_up), "w_down": bf(w_down),
        "post_ffn_norm": vec(post_ffn_norm),
    }


def _rope_tables(pos, dims):
    nope, rope = dims["nope"], dims["rope"]
    half = rope // 2
    inv = ROPE_THETA ** (-jnp.arange(half, dtype=F32) / half)
    ang = pos[:, None] * inv[None, :]
    c, s = jnp.cos(ang), jnp.sin(ang)
    n = pos.shape[0]
    z = lambda w: jnp.zeros((n, w), F32)
    pad = LANES - nope - rope
    cos_t = jnp.concatenate([jnp.ones((n, nope), F32), c, c, z(pad)], axis=-1)
    sin_a = jnp.concatenate([z(nope), -s, z(half + pad)], axis=-1)
    sin_b = jnp.concatenate([z(nope + half), s, z(pad)], axis=-1)
    return cos_t, sin_a, sin_b


def kernel(x_prompt, x_sample, cache_mla_latent, cache_mla_krope, state_gdn, state_gdn_conv, meta_tokens,
           pre_mix_norm, w_in, q_norm, kv_norm, w_uq, w_uk, w_uv, conv_w, a_log, dt_bias, gdn_norm, w_o,
           post_mix_norm, pre_ffn_norm, w_gate, w_up, w_down, post_ffn_norm):
    batch, seq, d = x_prompt.shape
    nb, t, _ = x_sample.shape
    depth, _, past, kvl = cache_mla_latent.shape
    n_meta = meta_tokens.shape[0]
    nh, nope = w_uk.shape[2], w_uk.shape[3]
    rope = cache_mla_krope.shape[-1]
    gh, dk, dv = state_gdn.shape[2:]
    cw = conv_w.shape[1]
    sm = nb * t
    fr0 = sm + ROW_TILE
    mt0 = fr0 - n_meta
    r = fr0 + seq
    assert batch == 1 and t == DEC_CHUNK and seq % (2 * ROW_TILE) == 0 and fr0 % (2 * ROW_TILE) == 0
    assert n_meta <= CHUNK and nh % 2 == 0 and nope + rope <= LANES and 2 * w_uv.shape[3] == LANES
    assert dk == LANES and dv == LANES and gh * CHUNK == GDN_STACK and cw - 1 <= SUBLANES and 2 * gh <= LANES
    assert sm % CHUNK == 0 and r % GDN_ROWS == 0 and n_meta + cw - 1 <= ROW_TILE
    assert nb % SATTN_BATCH == 0
    dims = dict(n_meta=n_meta, n_heads=nh, nope=nope, rope=rope, v_dim=w_uv.shape[3], ql=q_norm.shape[1],
                kvl=kvl, gdn_heads=gh, conv_w=cw, qkv_w=gh * (2 * dk + dv), z_w=gh * dv,
                q_scale=(nope + rope) ** -0.5 * math.log2(math.e), sm=sm, fr0=fr0)

    lw = _prepare_weights(pre_mix_norm, w_in, q_norm, kv_norm, w_uq, w_uk, w_uv, conv_w, a_log, dt_bias,
                          gdn_norm, w_o, post_mix_norm, pre_ffn_norm, w_gate, w_up, w_down, post_ffn_norm, dims)

    x = jnp.concatenate([x_sample.reshape(sm, d), jnp.zeros((ROW_TILE - n_meta, d), F32),
                         meta_tokens.astype(F32), x_prompt[0]], axis=0)
    pos = jnp.concatenate([jnp.tile(n_meta + past + jnp.arange(t), nb), jnp.zeros((ROW_TILE - n_meta,), jnp.int32),
                           jnp.arange(n_meta + seq)]).astype(F32)
    tables = _rope_tables(pos, dims)

    qkv_w = dims["qkv_w"]
    cache_pet = jnp.swapaxes(cache_mla_krope, 2, 3)
    outs = [[] for _ in range(8)]
    for l in range(depth):
        q, k, vt, ckv, kpe, qkv, z, ba = _front(x, tables, lw, l, dims)

        o_mla = _attention(q, k, vt, dims)
        o_mla = _sample_attention(q, ckv, kpe, cache_mla_latent, cache_pet, lw, l, o_mla, dims)

        o_gdn, s_prompt = _gdn_prompt(qkv, ba, z, lw, l, dims)
        xe_s = jnp.concatenate([state_gdn_conv[l].astype(F32), qkv[:sm].reshape(nb, t, qkv_w)], axis=1)
        o_gdn, s_sample = _gdn_sample(xe_s, ba, z, lw, l, state_gdn, o_gdn, dims)

        x = _back(x, o_mla, o_gdn, lw, l)

        kpe_r = kpe[:, nope:nope + rope]
        outs[0].append(ckv[mt0:][None])
        outs[1].append(kpe_r[mt0:][None])
        outs[2].append(s_prompt[None])
        outs[3].append(qkv[r - (cw - 1):][None])
        outs[4].append(ckv[:sm].reshape(nb, t, kvl))
        outs[5].append(kpe_r[:sm].reshape(nb, t, rope))
        outs[6].append(s_sample)
        outs[7].append(xe_s[:, t:])
    y_prompt = x[fr0:][None]
    y_sample = x[:sm].reshape(nb, t, d)
    return (y_prompt, y_sample) + tuple(jnp.stack(o) for o in outs)
```
